```python
import jax, jax.numpy as jnp
from jax import lax
import numpy as np

D_MODEL = 1024
BATCH = 16
SEQ = 2048
DEPTH = 1

CHUNK = 64
QUERY_BLOCK = 128
HEAD_DIM = 64
SB_HEADS = 8
RW_HEADS = 8
BRANCH_WIDTH = 512
N_BRANCHES = 2
DECAY_LORA = 64
ICLR_LORA = 64
GATE_LORA = 160
D_FF = 2816
CONV_WIDTH = 3
NORM_EPS = 1e-6
GN_EPS = 64e-5

SB_COLS = 3 * BRANCH_WIDTH
RW_COLS = 3 * BRANCH_WIDTH + DECAY_LORA + ICLR_LORA + GATE_LORA
GATE_COLS = N_BRANCHES * D_MODEL
IN_COLS = SB_COLS + RW_COLS + GATE_COLS

kernel_name = "stickbreak_rwkv7_gated_hybrid"


def rmsnorm(x, g):
    x32 = x.astype(jnp.float32)
    y = x32 * lax.rsqrt(jnp.mean(x32 * x32, axis=-1, keepdims=True) + NORM_EPS) * g
    return y.astype(x.dtype)


def stick_breaking_attention(q, k, v):
    seq, dh = q.shape[2], q.shape[3]
    scale = dh ** -0.5
    outs = []
    for blk in range(seq // QUERY_BLOCK):
        q0 = blk * QUERY_BLOCK
        q1 = q0 + QUERY_BLOCK
        qb = q[:, :, q0:q1]
        kb = k[:, :, :q1]
        vb = v[:, :, :q1]
        z = jnp.einsum('bhqd,bhkd->bhqk', qb, kb) * scale
        t_pos = q0 + jnp.arange(QUERY_BLOCK)
        s_pos = jnp.arange(q1)
        strict = s_pos[None, :] < t_pos[:, None]
        log_keep = jnp.where(strict, jax.nn.log_sigmoid(-z), 0.0)
        after = lax.cumsum(log_keep, axis=3, reverse=True) - log_keep
        weights = jnp.where(strict, jnp.exp(jax.nn.log_sigmoid(z) + after), 0.0)
        outs.append(jnp.einsum('bhqk,bhkd->bhqd', weights, vb))
    return jnp.concatenate(outs, axis=2)


def token_shift(p, mu):
    prev = jnp.pad(p, ((0, 0), (1, 0), (0, 0)))[:, :-1]
    return p + (prev - p) * mu


def rwkv7_recurrence(r, w, k, v, a, b):
    bsz, _, h, n = r.shape

    def step(state, inp):
        r_t, w_t, k_t, v_t, a_t, b_t = inp
        sa = jnp.einsum('bhvk,bhk->bhv', state, a_t)
        state = (state * w_t[:, :, None, :] + sa[..., None] * b_t[:, :, None, :]
                 + v_t[..., None] * k_t[:, :, None, :])
        y = jnp.einsum('bhvk,bhk->bhv', state, r_t)
        return state, y

    xs = tuple(jnp.moveaxis(t, 1, 0) for t in (r, w, k, v, a, b))
    s0 = jnp.zeros((bsz, h, n, n), jnp.float32)
    _, ys = lax.scan(step, s0, xs)
    return jnp.moveaxis(ys, 0, 1)


def causal_depthwise_conv(u, w, bias):
    c = u.shape[-1]
    y = lax.conv_general_dilated(
        u, w[:, None, :].astype(u.dtype), window_strides=(1,),
        padding=[(CONV_WIDTH - 1, 0)], dimension_numbers=('NWC', 'WIO', 'NWC'),
        feature_group_count=c)
    return y + bias


def hybrid_layer(x, attn_norm_g, w_in, q_norm_g, k_norm_g, rwkv_shift_mu, decay_base, decay_up,
                 iclr_base, iclr_up, out_gate_up, key_k, key_a, bonus_rk, group_norm_w,
                 group_norm_b, branch_gate_b, w_branch, w_out, ffn_norm_g, w_ffn_up,
                 ffn_conv_w, ffn_conv_b, w_ffn_down):
    bsz, seq, _ = x.shape
    f32 = jnp.float32
    heads = lambda t: t.reshape(bsz, seq, -1, HEAD_DIM)

    h = rmsnorm(x, attn_norm_g)
    proj = h @ w_in
    p_sb, p_rw, p_gate = jnp.split(proj, [SB_COLS, SB_COLS + RW_COLS], axis=-1)

    q_sb, k_sb, v_sb = jnp.split(p_sb, 3, axis=-1)
    q_sb = rmsnorm(heads(q_sb), q_norm_g).astype(f32).transpose(0, 2, 1, 3)
    k_sb = rmsnorm(heads(k_sb), k_norm_g).astype(f32).transpose(0, 2, 1, 3)
    v_sb = heads(v_sb).astype(f32).transpose(0, 2, 1, 3)
    o_sb = stick_breaking_attention(q_sb, k_sb, v_sb)
    o_sb = o_sb.transpose(0, 2, 1, 3).reshape(bsz, seq, BRANCH_WIDTH).astype(x.dtype)

    xs = token_shift(p_rw, rwkv_shift_mu).astype(f32)
    c1 = BRANCH_WIDTH
    offs = [c1, 2 * c1, 3 * c1, 3 * c1 + DECAY_LORA, 3 * c1 + DECAY_LORA + ICLR_LORA]
    r, k, v, w_lo, a_lo, g_lo = jnp.split(xs, offs, axis=-1)
    w_log = -jax.nn.softplus(-(decay_base + jnp.tanh(w_lo) @ decay_up)) - 0.5
    decay = jnp.exp(-jnp.exp(w_log))
    iclr = jax.nn.sigmoid(iclr_base + a_lo @ iclr_up)
    gate = jax.nn.sigmoid(g_lo) @ out_gate_up
    kk = heads(k * key_k)
    kk = kk / jnp.maximum(jnp.linalg.norm(kk, axis=-1, keepdims=True), 1e-12)
    k = k * (1.0 + (iclr - 1.0) * key_a)
    rh, wh, kh, vh, ah = heads(r), heads(decay), heads(k), heads(v), heads(iclr)
    y = rwkv7_recurrence(rh, wh, kh, vh, -kk, kk * ah)
    mu = jnp.mean(y, axis=-1, keepdims=True)
    var = jnp.mean(jnp.square(y - mu), axis=-1, keepdims=True)
    y = ((y - mu) * lax.rsqrt(var + GN_EPS)).reshape(bsz, seq, BRANCH_WIDTH)
    y = y * group_norm_w + group_norm_b
    bonus = jnp.sum(rh * kh * bonus_rk, axis=-1, keepdims=True) * vh
    o_rw = ((y + bonus.reshape(bsz, seq, BRANCH_WIDTH)) * gate).astype(x.dtype)

    branches = jnp.stack([o_sb, o_rw], axis=2)
    up = jnp.einsum('bsnc,ncd->bsnd', branches, w_branch)
    gates = jax.nn.sigmoid(p_gate.reshape(bsz, seq, N_BRANCHES, D_MODEL) + branch_gate_b)
    mixed = jnp.sum(gates * up, axis=2)
    x = x + mixed @ w_out

    h2 = rmsnorm(x, ffn_norm_g)
    u = causal_depthwise_conv(h2 @ w_ffn_up, ffn_conv_w, ffn_conv_b)
    u_gate, u_val = jnp.split(u, 2, axis=-1)
    x = x + (jax.nn.silu(u_gate) * u_val) @ w_ffn_down
    return x


def setup_inputs(seed: int = 0) -> dict:
    key = jax.random.key(seed)
    ks = jax.random.split(key, 26)
    L = DEPTH

    def nrm(k, shape, scale):
        return scale * jax.random.normal(k, shape, jnp.float32)

    def gain(k, shape):
        return 1.0 + 0.02 * jax.random.normal(k, shape, jnp.float32)

    return {
        "x": jax.random.normal(ks[0], (BATCH, SEQ, D_MODEL), jnp.float32),
        "attn_norm_g": gain(ks[1], (L, D_MODEL)),
        "w_in": nrm(ks[2], (L, D_MODEL, IN_COLS), D_MODEL ** -0.5),
        "q_norm_g": gain(ks[3], (L, HEAD_DIM)),
        "k_norm_g": gain(ks[4], (L, HEAD_DIM)),
        "rwkv_shift_mu": jax.random.uniform(ks[5], (L, RW_COLS), jnp.float32),
        "decay_base": jax.random.uniform(ks[6], (L, BRANCH_WIDTH), jnp.float32, minval=-6.0, maxval=1.0),
        "decay_up": nrm(ks[7], (L, DECAY_LORA, BRANCH_WIDTH), 0.5 * DECAY_LORA ** -0.5),
        "iclr_base": jax.random.uniform(ks[8], (L, BRANCH_WIDTH), jnp.float32, minval=-1.0, maxval=1.0),
        "iclr_up": nrm(ks[9], (L, ICLR_LORA, BRANCH_WIDTH), 0.5 * ICLR_LORA ** -0.5),
        "out_gate_up": nrm(ks[10], (L, GATE_LORA, BRANCH_WIDTH), GATE_LORA ** -0.5),
        "key_k": 0.85 + 0.02 * jax.random.normal(ks[11], (L, BRANCH_WIDTH), jnp.float32),
        "key_a": gain(ks[12], (L, BRANCH_WIDTH)),
        "bonus_rk": nrm(ks[13], (L, RW_HEADS, HEAD_DIM), 0.1),
        "group_norm_w": gain(ks[14], (L, BRANCH_WIDTH)),
        "group_norm_b": nrm(ks[15], (L, BRANCH_WIDTH), 0.02),
        "branch_gate_b": nrm(ks[16], (L, N_BRANCHES, D_MODEL), 0.02),
        "w_branch": nrm(ks[17], (L, N_BRANCHES, BRANCH_WIDTH, D_MODEL), BRANCH_WIDTH ** -0.5),
        "w_out": nrm(ks[18], (L, D_MODEL, D_MODEL), D_MODEL ** -0.5),
        "ffn_norm_g": gain(ks[19], (L, D_MODEL)),
        "w_ffn_up": nrm(ks[20], (L, D_MODEL, 2 * D_FF), D_MODEL ** -0.5),
        "ffn_conv_w": nrm(ks[21], (L, CONV_WIDTH, 2 * D_FF), CONV_WIDTH ** -0.5),
        "ffn_conv_b": nrm(ks[22], (L, 2 * D_FF), 0.02),
        "w_ffn_down": nrm(ks[23], (L, D_FF, D_MODEL), D_FF ** -0.5),
    }


def reference(x, attn_norm_g, w_in, q_norm_g, k_norm_g, rwkv_shift_mu, decay_base, decay_up,
              iclr_base, iclr_up, out_gate_up, key_k, key_a, bonus_rk, group_norm_w,
              group_norm_b, branch_gate_b, w_branch, w_out, ffn_norm_g, w_ffn_up,
              ffn_conv_w, ffn_conv_b, w_ffn_down):
    for layer in range(DEPTH):
        x = hybrid_layer(
            x, attn_norm_g[layer], w_in[layer], q_norm_g[layer], k_norm_g[layer],
            rwkv_shift_mu[layer], decay_base[layer], decay_up[layer], iclr_base[layer],
            iclr_up[layer], out_gate_up[layer], key_k[layer], key_a[layer], bonus_rk[layer],
            group_norm_w[layer], group_norm_b[layer], branch_gate_b[layer], w_branch[layer],
            w_out[layer], ffn_norm_g[layer], w_ffn_up[layer], ffn_conv_w[layer],
            ffn_conv_b[layer], w_ffn_down[layer])
    return x
```

```python
import functools

import jax
import jax.numpy as jnp
from jax import lax
from jax.experimental import pallas as pl
from jax.experimental.pallas import tpu as pltpu

F32 = jnp.float32
BF16 = jnp.bfloat16

HEAD_DIM = 64
N_HEADS = 8
BRANCH_WIDTH = N_HEADS * HEAD_DIM
LANES = 128
HEADS_PER_GROUP = LANES // HEAD_DIM
N_GROUPS = BRANCH_WIDTH // LANES
DECAY_LORA = 64
ICLR_LORA = 64
GATE_LORA = 160
LORA_W_PAD = 128
LORA_A_PAD = 128
LORA_G_PAD = 256
RW_COLS_PAD = 3 * BRANCH_WIDTH + LORA_W_PAD + LORA_A_PAD + LORA_G_PAD
CONV_WIDTH = 3
NORM_EPS = 1e-6
GN_EPS = 64e-5
CHUNK = 64
VMEM_LIMIT = 56 * 1024 * 1024


def _dot(a, b):
    return jnp.dot(a, b, preferred_element_type=F32)


def _dot_t(a, b):
    return lax.dot_general(a, b, (((1,), (1,)), ((), ())), preferred_element_type=F32)


def _split(a):
    hi = a.astype(BF16)
    lo = (a - hi.astype(F32)).astype(BF16)
    return hi, lo


def _dot3(a, b):
    ah, al = _split(a)
    bh, bl = _split(b)
    return _dot(ah, bh) + _dot(ah, bl) + _dot(al, bh)


def _dot2l(a, b_bf16):
    ah, al = _split(a)
    return _dot(ah, b_bf16) + _dot(al, b_bf16)


def _dot2r(a_bf16, b):
    bh, bl = _split(b)
    return _dot(a_bf16, bh) + _dot(a_bf16, bl)


def _sigmoid(x):
    return 1.0 / (1.0 + jnp.exp(-x))


def _softplus(x):
    return jnp.maximum(x, 0.0) + jnp.log(1.0 + jnp.exp(-jnp.abs(x)))


def _inproj_kernel(x_ref, g_ref, wsb_ref, wrw_ref, wgate_ref, mu_ref,
                   psb_ref, xs_ref, pgate_ref, carry_ref, *, tiles_per_seq):
    i = pl.program_id(0)
    x = x_ref[...]
    h = x * lax.rsqrt(jnp.mean(x * x, axis=-1, keepdims=True) + NORM_EPS) * g_ref[...]
    hb = h.astype(BF16)
    psb_ref[...] = _dot(hb, wsb_ref[...])
    pgate_ref[...] = _dot(hb, wgate_ref[...])
    p = _dot(hb, wrw_ref[...])
    first = (i % tiles_per_seq) == 0
    prev_last = jnp.where(first, 0.0, carry_ref[7:8, :])
    rolled = pltpu.roll(p, 1, 0)
    row = lax.broadcasted_iota(jnp.int32, p.shape, 0)
    prev = jnp.where(row == 0, prev_last, rolled)
    xs_ref[...] = p + (prev - p) * mu_ref[...]
    carry_ref[...] = p[p.shape[0] - 8:, :]


def _in_projection(x2, g, w_sb, w_rw, w_gate, mu, seq, tm):
    t, d = x2.shape
    n_sb, n_rw, n_gate = w_sb.shape[1], w_rw.shape[1], w_gate.shape[1]
    const = lambda i: (0, 0)
    row = lambda i: (i, 0)
    return pl.pallas_call(
        functools.partial(_inproj_kernel, tiles_per_seq=seq // tm),
        grid=(t // tm,),
        in_specs=[
            pl.BlockSpec((tm, d), row),
            pl.BlockSpec((1, d), const),
            pl.BlockSpec((d, n_sb), const),
            pl.BlockSpec((d, n_rw), const),
            pl.BlockSpec((d, n_gate), const),
            pl.BlockSpec((1, n_rw), const),
        ],
        out_specs=[
            pl.BlockSpec((tm, n_sb), row),
            pl.BlockSpec((tm, n_rw), row),
            pl.BlockSpec((tm, n_gate), row),
        ],
        out_shape=[
            jax.ShapeDtypeStruct((t, n_sb), F32),
            jax.ShapeDtypeStruct((t, n_rw), F32),
            jax.ShapeDtypeStruct((t, n_gate), F32),
        ],
        scratch_shapes=[pltpu.VMEM((8, n_rw), F32)],
        compiler_params=pltpu.CompilerParams(
            dimension_semantics=("arbitrary",), vmem_limit_bytes=VMEM_LIMIT),
        name="in_projection",
    )(x2, g, w_sb, w_rw, w_gate, mu)


def _head_rmsnorm(x, g_pair, lane_lo):
    sq = x * x
    s_lo = jnp.sum(jnp.where(lane_lo, sq, 0.0), axis=-1, keepdims=True)
    s_all = jnp.sum(sq, axis=-1, keepdims=True)
    ms = jnp.where(lane_lo, s_lo, s_all - s_lo) * (1.0 / HEAD_DIM)
    return x * lax.rsqrt(ms + NORM_EPS) * g_pair


def _sb_attn_kernel(q_ref, k_ref, v_ref, qg_ref, kg_ref, tri_ref, o_ref,
                    kn_ref, vn_ref, *, blk):
    qi = pl.program_id(2)
    lane_lo_k = lax.broadcasted_iota(jnp.int32, k_ref.shape, 1) < HEAD_DIM

    @pl.when(qi == 0)
    def _():
        kn = _head_rmsnorm(k_ref[...], kg_ref[...], lane_lo_k).astype(BF16)
        vb = v_ref[...].astype(BF16)
        for hh in range(HEADS_PER_GROUP):
            kn_ref[hh] = kn[:, hh * HEAD_DIM:(hh + 1) * HEAD_DIM]
            vn_ref[hh] = vb[:, hh * HEAD_DIM:(hh + 1) * HEAD_DIM]

    lane_lo_q = lax.broadcasted_iota(jnp.int32, q_ref.shape, 1) < HEAD_DIM
    qn = (_head_rmsnorm(q_ref[...], qg_ref[...], lane_lo_q) * (HEAD_DIM ** -0.5)).astype(BF16)
    tri = tri_ref[...]
    t_idx = lax.broadcasted_iota(jnp.int32, (blk, blk), 0)
    s_idx = lax.broadcasted_iota(jnp.int32, (blk, blk), 1)
    strict = s_idx < t_idx

    def block(qh, hh, start, carry, acc, diag):
        kb = kn_ref[hh, pl.ds(start, blk), :]
        vb = vn_ref[hh, pl.ds(start, blk), :]
        z = _dot_t(qh, kb)
        sp = jnp.log(1.0 + jnp.exp(-jnp.abs(z)))
        ls_pos = jnp.minimum(z, 0.0) - sp
        lk = ls_pos - z
        if diag:
            lk = jnp.where(strict, lk, 0.0)
        after = _dot2l(lk, tri) + carry
        w = jnp.exp(ls_pos + after)
        if diag:
            w = jnp.where(strict, w, 0.0)
        acc = acc + _dot(w.astype(BF16), vb)
        carry = carry + jnp.sum(lk, axis=-1, keepdims=True)
        return carry, acc

    outs = []
    for hh in range(HEADS_PER_GROUP):
        qh = qn[:, hh * HEAD_DIM:(hh + 1) * HEAD_DIM]
        carry0 = jnp.zeros((blk, 1), F32)
        acc0 = jnp.zeros((blk, HEAD_DIM), F32)
        carry, acc = block(qh, hh, pl.multiple_of(qi * blk, blk), carry0, acc0, True)

        def body(j, ca, qh=qh, hh=hh):
            start = pl.multiple_of((qi - 1 - j) * blk, blk)
            return block(qh, hh, start, ca[0], ca[1], False)

        carry, acc = lax.fori_loop(0, qi, body, (carry, acc))
        outs.append(acc)
    o_ref[...] = jnp.concatenate(outs, axis=-1)


def _sb_attention(p_sb3, qg_pair, kg_pair, tri, blk):
    bsz, seq, _ = p_sb3.shape
    nq = seq // blk
    kblk0 = BRANCH_WIDTH // LANES
    return pl.pallas_call(
        functools.partial(_sb_attn_kernel, blk=blk),
        grid=(bsz, N_GROUPS, nq),
        in_specs=[
            pl.BlockSpec((None, blk, LANES), lambda b, g, i: (b, i, g)),
            pl.BlockSpec((None, seq, LANES), lambda b, g, i: (b, 0, kblk0 + g)),
            pl.BlockSpec((None, seq, LANES), lambda b, g, i: (b, 0, 2 * kblk0 + g)),
            pl.BlockSpec((1, LANES), lambda b, g, i: (0, 0)),
            pl.BlockSpec((1, LANES), lambda b, g, i: (0, 0)),
            pl.BlockSpec((blk, blk), lambda b, g, i: (0, 0)),
        ],
        out_specs=pl.BlockSpec((None, blk, LANES), lambda b, g, i: (b, i, g)),
        out_shape=jax.ShapeDtypeStruct((bsz, seq, BRANCH_WIDTH), F32),
        scratch_shapes=[
            pltpu.VMEM((HEADS_PER_GROUP, seq, HEAD_DIM), BF16),
            pltpu.VMEM((HEADS_PER_GROUP, seq, HEAD_DIM), BF16),
        ],
        compiler_params=pltpu.CompilerParams(
            dimension_semantics=("parallel", "parallel", "arbitrary"),
            vmem_limit_bytes=VMEM_LIMIT),
        name="sb_attention",
    )(p_sb3, p_sb3, p_sb3, qg_pair, kg_pair, tri)


def _rwkv_kernel(xs_ref, dbase_ref, dup_ref, ibase_ref, iup_ref, gup_ref, keyk_ref,
                 keya_ref, bonus_ref, gnw_ref, gnb_ref, seg_ref, tri_ref, o_ref,
                 h_ref, *, n_sub):
    c = CHUNK
    bw = BRANCH_WIDTH

    @pl.when(pl.program_id(1) == 0)
    def _():
        h_ref[...] = jnp.zeros_like(h_ref)

    xs = xs_ref[...]
    r = xs[:, 0:bw]
    k = xs[:, bw:2 * bw]
    v = xs[:, 2 * bw:3 * bw]
    o0 = 3 * bw
    w_lo = xs[:, o0:o0 + LORA_W_PAD]
    a_lo = xs[:, o0 + LORA_W_PAD:o0 + LORA_W_PAD + LORA_A_PAD]
    g_lo = xs[:, o0 + LORA_W_PAD + LORA_A_PAD:]
    seg = seg_ref[...]

    dw = dbase_ref[...] + _dot3(jnp.tanh(w_lo), dup_ref[...])
    w_log = -_softplus(-dw) - 0.5
    lw = -jnp.exp(w_log)
    iclr = _sigmoid(ibase_ref[...] + _dot3(a_lo, iup_ref[...]))
    gate = _dot3(_sigmoid(g_lo), gup_ref[...])
    kk0 = k * keyk_ref[...]
    kk = kk0 / jnp.maximum(jnp.sqrt(_dot2l(kk0 * kk0, seg)), 1e-12)
    kmod = k * (1.0 + (iclr - 1.0) * keya_ref[...])
    a = -kk
    b = kk * iclr
    bonus = _dot2l(r * kmod * bonus_ref[...], seg) * v

    tri = tri_ref[...]
    row = lax.broadcasted_iota(jnp.int32, (LANES, LANES), 0)
    col = lax.broadcasted_iota(jnp.int32, (LANES, LANES), 1)
    same_head = (row // c) == (col // c)
    strict_lo = jnp.logical_and(same_head, (col % c) < (row % c))
    incl_lo = jnp.logical_and(same_head, (col % c) <= (row % c))
    eye = row == col
    lane = lax.broadcasted_iota(jnp.int32, (c, LANES), 1)
    head_masks = [(lane // HEAD_DIM) == hh for hh in range(HEADS_PER_GROUP)]

    def stack(x):
        return jnp.concatenate([jnp.where(m, x, 0.0) for m in head_masks], axis=0)

    ys = []
    for s in range(n_sub):
        sl = slice(s * c, (s + 1) * c)
        lw_s = lw[sl]
        cum = _dot2r(tri, lw_s)
        tot = cum[c - 1:c, :]
        e_cum = jnp.exp(cum)
        e_prev = jnp.exp(cum - lw_s)
        e_inv = jnp.exp(-cum)
        e_rem = jnp.exp(tot - cum)
        p_tot = jnp.exp(tot)
        a_t = a[sl] * e_prev
        r_t = r[sl] * e_cum
        b_t = b[sl] * e_inv
        k_t = kmod[sl] * e_inv
        b_h = b[sl] * e_rem
        k_h = kmod[sl] * e_rem
        v_s = v[sl]
        y_groups = []
        for g in range(N_GROUPS):
            ls = slice(g * LANES, (g + 1) * LANES)
            a2, r2, b2, k2 = stack(a_t[:, ls]), stack(r_t[:, ls]), stack(b_t[:, ls]), stack(k_t[:, ls])
            bh2, kh2, v2 = stack(b_h[:, ls]), stack(k_h[:, ls]), stack(v_s[:, ls])
            ar2 = jnp.concatenate([a2, r2], axis=0)
            bk2 = jnp.concatenate([b2, k2], axis=0)
            sc = _dot_t(ar2, bk2)
            l_ab = jnp.where(strict_lo, sc[:2 * c, :2 * c], 0.0)
            l_ak = jnp.where(strict_lo, sc[:2 * c, 2 * c:], 0.0)
            m_rb = jnp.where(incl_lo, sc[2 * c:, :2 * c], 0.0)
            m_rk = jnp.where(incl_lo, sc[2 * c:, 2 * c:], 0.0)
            t_inv = jnp.where(eye, 1.0, l_ab)
            l_pow = l_ab
            for _ in range(5):
                l_pow = _dot3(l_pow, l_pow)
                t_inv = t_inv + _dot3(t_inv, l_pow)
            x_loc = _dot(l_ak, v2)
            zw = _dot(t_inv, jnp.concatenate([a2, x_loc], axis=1))
            ap2 = zw[:, :LANES]
            wt2 = zw[:, LANES:]
            rh2 = r2 + _dot(m_rb, ap2)
            yl2 = _dot(m_rb, wt2) + _dot(m_rk, v2)
            bh2_t = bh2.T
            m_mat = _dot(bh2_t, ap2) + jnp.where(eye, p_tot[:, ls], 0.0)
            n_mat = _dot(bh2_t, wt2) + _dot(kh2.T, v2)
            h0 = h_ref[g]
            y2 = _dot3(rh2, h0) + yl2
            h_ref[g] = _dot3(m_mat, h0) + n_mat
            y_groups.append(y2[:c] + y2[c:])
        ys.append(jnp.concatenate(y_groups, axis=-1))
    y = jnp.concatenate(ys, axis=0) if n_sub > 1 else ys[0]

    inv_n = 1.0 / HEAD_DIM
    mu = _dot2l(y, seg) * inv_n
    d = y - mu
    var = _dot2l(d * d, seg) * inv_n
    yn = d * lax.rsqrt(var + GN_EPS) * gnw_ref[...] + gnb_ref[...]
    o_ref[...] = (yn + bonus) * gate


def _rwkv_mix(xs3, dbase, dup, ibase, iup, gup, keyk, keya, bonus, gnw, gnb, seg, tri, tc):
    bsz, seq, ncol = xs3.shape
    const = lambda b, i: (0, 0)
    vec = pl.BlockSpec((1, BRANCH_WIDTH), const)
    return pl.pallas_call(
        functools.partial(_rwkv_kernel, n_sub=tc // CHUNK),
        grid=(bsz, seq // tc),
        in_specs=[
            pl.BlockSpec((None, tc, ncol), lambda b, i: (b, i, 0)),
            vec,
            pl.BlockSpec(dup.shape, const),
            vec,
            pl.BlockSpec(iup.shape, const),
            pl.BlockSpec(gup.shape, const),
            vec, vec, vec, vec, vec,
            pl.BlockSpec(seg.shape, const),
            pl.BlockSpec(tri.shape, const),
        ],
        out_specs=pl.BlockSpec((None, tc, BRANCH_WIDTH), lambda b, i: (b, i, 0)),
        out_shape=jax.ShapeDtypeStruct((bsz, seq, BRANCH_WIDTH), F32),
        scratch_shapes=[pltpu.VMEM((N_GROUPS, LANES, LANES), F32)],
        compiler_params=pltpu.CompilerParams(
            dimension_semantics=("parallel", "arbitrary"), vmem_limit_bytes=VMEM_LIMIT),
        name="rwkv_mix",
    )(xs3, dbase, dup, ibase, iup, gup, keyk, keya, bonus, gnw, gnb, seg, tri)


def _merge_kernel(x_ref, osb_ref, orw_ref, pg_ref, bg_ref, wb0_ref, wb1_ref, wout_ref,
                  g2_ref, x1_ref, h2_ref):
    d = x_ref.shape[1]
    up0 = _dot(osb_ref[...].astype(BF16), wb0_ref[...])
    up1 = _dot(orw_ref[...].astype(BF16), wb1_ref[...])
    gates = _sigmoid(pg_ref[...] + bg_ref[...])
    mixed = gates[:, :d] * up0 + gates[:, d:] * up1
    x1 = x_ref[...] + _dot(mixed.astype(BF16), wout_ref[...])
    x1_ref[...] = x1
    h2 = x1 * lax.rsqrt(jnp.mean(x1 * x1, axis=-1, keepdims=True) + NORM_EPS) * g2_ref[...]
    h2_ref[...] = h2.astype(BF16)


def _merge(x2, o_sb, o_rw, p_gate, bg, wb0, wb1, w_out, g2, tm):
    t, d = x2.shape
    const = lambda i: (0, 0)
    row = lambda i: (i, 0)
    return pl.pallas_call(
        _merge_kernel,
        grid=(t // tm,),
        in_specs=[
            pl.BlockSpec((tm, d), row),
            pl.BlockSpec((tm, BRANCH_WIDTH), row),
            pl.BlockSpec((tm, BRANCH_WIDTH), row),
            pl.BlockSpec((tm, 2 * d), row),
            pl.BlockSpec((1, 2 * d), const),
            pl.BlockSpec(wb0.shape, const),
            pl.BlockSpec(wb1.shape, const),
            pl.BlockSpec(w_out.shape, const),
            pl.BlockSpec((1, d), const),
        ],
        out_specs=[pl.BlockSpec((tm, d), row), pl.BlockSpec((tm, d), row)],
        out_shape=[jax.ShapeDtypeStruct((t, d), F32), jax.ShapeDtypeStruct((t, d), BF16)],
        compiler_params=pltpu.CompilerParams(
            dimension_semantics=("parallel",), vmem_limit_bytes=VMEM_LIMIT),
        name="merge_outproj",
    )(x2, o_sb, o_rw, p_gate, bg, wb0, wb1, w_out, g2)


def _ffn_kernel(h2_ref, x1_ref, wup_ref, cw_ref, cb_ref, wdown_ref, o_ref, carry_ref,
                *, tiles_per_seq, d_ff, nc):
    i = pl.program_id(0)
    first = (i % tiles_per_seq) == 0
    h2 = h2_ref[...]
    tm = h2.shape[0]
    row = lax.broadcasted_iota(jnp.int32, (tm, nc), 0)

    def conv(u, col0):
        cs = slice(col0, col0 + nc)
        old = jnp.where(first, 0.0, carry_ref[:, cs])
        p1 = jnp.where(row == 0, old[7:8, :], pltpu.roll(u, 1, 0))
        p2 = jnp.where(row == 0, old[6:7, :],
                       jnp.where(row == 1, old[7:8, :], pltpu.roll(u, 2, 0)))
        carry_ref[:, cs] = u[tm - 8:, :]
        cw = cw_ref[:, cs]
        return cw[0:1] * p2 + cw[1:2] * p1 + cw[2:3] * u + cb_ref[:, cs]

    acc = x1_ref[...]
    for ch in range(d_ff // nc):
        cg = conv(_dot(h2, wup_ref[:, ch * nc:(ch + 1) * nc]), ch * nc)
        cv = conv(_dot(h2, wup_ref[:, d_ff + ch * nc:d_ff + (ch + 1) * nc]), d_ff + ch * nc)
        act = cg * _sigmoid(cg) * cv
        acc = acc + _dot(act.astype(BF16), wdown_ref[ch * nc:(ch + 1) * nc, :])
    o_ref[...] = acc


def _ffn(h2, x1, w_up, conv_w, conv_b, w_down, seq, tm, nc):
    t, d = x1.shape
    d_ff = w_down.shape[0]
    const = lambda i: (0, 0)
    row = lambda i: (i, 0)
    return pl.pallas_call(
        functools.partial(_ffn_kernel, tiles_per_seq=seq // tm, d_ff=d_ff, nc=nc),
        grid=(t // tm,),
        in_specs=[
            pl.BlockSpec((tm, d), row),
            pl.BlockSpec((tm, d), row),
            pl.BlockSpec(w_up.shape, const),
            pl.BlockSpec(conv_w.shape, const),
            pl.BlockSpec(conv_b.shape, const),
            pl.BlockSpec(w_down.shape, const),
        ],
        out_specs=pl.BlockSpec((tm, d), row),
        out_shape=jax.ShapeDtypeStruct((t, d), F32),
        scratch_shapes=[pltpu.VMEM((8, 2 * d_ff), F32)],
        compiler_params=pltpu.CompilerParams(
            dimension_semantics=("arbitrary",), vmem_limit_bytes=VMEM_LIMIT),
        name="ffn",
    )(h2, x1, w_up, conv_w, conv_b, w_down)


def _pad_cols(w, n):
    return jnp.pad(w, ((0, 0), (0, n - w.shape[1])))


def _pad_rows(w, n):
    return jnp.pad(w, ((0, n - w.shape[0]), (0, 0)))


def _layer(x, attn_norm_g, w_in, q_norm_g, k_norm_g, rwkv_shift_mu, decay_base, decay_up,
           iclr_base, iclr_up, out_gate_up, key_k, key_a, bonus_rk, group_norm_w,
           group_norm_b, branch_gate_b, w_branch, w_out, ffn_norm_g, w_ffn_up,
           ffn_conv_w, ffn_conv_b, w_ffn_down):
    bsz, seq, d = x.shape
    t = bsz * seq
    bw = BRANCH_WIDTH
    sb_cols = 3 * bw
    rw_cols = 3 * bw + DECAY_LORA + ICLR_LORA + GATE_LORA
    x2 = x.reshape(t, d)

    w_sb = w_in[:, :sb_cols].astype(BF16)
    w_rw_raw = w_in[:, sb_cols:sb_cols + rw_cols]
    mu_raw = rwkv_shift_mu.reshape(1, rw_cols)

    def regroup(m):
        o = 3 * bw
        return jnp.concatenate([
            m[:, :o],
            _pad_cols(m[:, o:o + DECAY_LORA], LORA_W_PAD),
            _pad_cols(m[:, o + DECAY_LORA:o + DECAY_LORA + ICLR_LORA], LORA_A_PAD),
            _pad_cols(m[:, o + DECAY_LORA + ICLR_LORA:], LORA_G_PAD),
        ], axis=1)

    w_rw = regroup(w_rw_raw).astype(BF16)
    mu = regroup(mu_raw)
    w_gate = w_in[:, sb_cols + rw_cols:].astype(BF16)

    p_sb, xs, p_gate = _in_projection(
        x2, attn_norm_g.reshape(1, d), w_sb, w_rw, w_gate, mu, seq, tm=256)

    blk = 256
    tri_sb = (jnp.arange(blk)[:, None] > jnp.arange(blk)[None, :]).astype(BF16)
    qg_pair = jnp.tile(q_norm_g.reshape(1, HEAD_DIM), (1, HEADS_PER_GROUP))
    kg_pair = jnp.tile(k_norm_g.reshape(1, HEAD_DIM), (1, HEADS_PER_GROUP))
    o_sb = _sb_attention(p_sb.reshape(bsz, seq, sb_cols), qg_pair, kg_pair, tri_sb, blk)

    head_id = jnp.arange(bw) // HEAD_DIM
    seg = (head_id[:, None] == head_id[None, :]).astype(BF16)
    tri_rw = (jnp.arange(CHUNK)[:, None] >= jnp.arange(CHUNK)[None, :]).astype(BF16)
    vec = lambda p: p.reshape(1, bw)
    o_rw = _rwkv_mix(
        xs.reshape(bsz, seq, RW_COLS_PAD), vec(decay_base), _pad_rows(decay_up, LORA_W_PAD),
        vec(iclr_base), _pad_rows(iclr_up, LORA_A_PAD), _pad_rows(out_gate_up, LORA_G_PAD),
        vec(key_k), vec(key_a), vec(bonus_rk), vec(group_norm_w), vec(group_norm_b),
        seg, tri_rw, tc=128)

    x1, h2 = _merge(
        x2, o_sb.reshape(t, bw), o_rw.reshape(t, bw), p_gate,
        branch_gate_b.reshape(1, 2 * d), w_branch[0].astype(BF16), w_branch[1].astype(BF16),
        w_out.astype(BF16), ffn_norm_g.reshape(1, d), tm=512)

    out = _ffn(h2, x1, w_ffn_up.astype(BF16), ffn_conv_w, ffn_conv_b.reshape(1, -1),
               w_ffn_down.astype(BF16), seq, tm=256, nc=256)
    return out.reshape(bsz, seq, d)


def kernel(x, attn_norm_g, w_in, q_norm_g, k_norm_g, rwkv_shift_mu, decay_base, decay_up,
           iclr_base, iclr_up, out_gate_up, key_k, key_a, bonus_rk, group_norm_w,
           group_norm_b, branch_gate_b, w_branch, w_out, ffn_norm_g, w_ffn_up, ffn_conv_w,
           ffn_conv_b, w_ffn_down):
    for layer in range(attn_norm_g.shape[0]):
        x = _layer(
            x, attn_norm_g[layer], w_in[layer], q_norm_g[layer], k_norm_g[layer],
            rwkv_shift_mu[layer], decay_base[layer], decay_up[layer], iclr_base[layer],
            iclr_up[layer], out_gate_up[layer], key_k[layer], key_a[layer], bonus_rk[layer],
            group_norm_w[layer], group_norm_b[layer], branch_gate_b[layer], w_branch[layer],
            w_out[layer], ffn_norm_g[layer], w_ffn_up[layer], ffn_conv_w[layer],
            ffn_conv_b[layer], w_ffn_down[layer])
    return x
```

```python
import functools

import jax
import jax.numpy as jnp
from jax import lax
from jax.experimental import pallas as pl
from jax.experimental.pallas import tpu as pltpu

F32 = jnp.float32
BF16 = jnp.bfloat16

HEAD_DIM = 64
N_HEADS = 8
BRANCH_WIDTH = N_HEADS * HEAD_DIM
LANES = 128
HEADS_PER_GROUP = LANES // HEAD_DIM
N_GROUPS = BRANCH_WIDTH // LANES
DECAY_LORA = 64
ICLR_LORA = 64
GATE_LORA = 160
LORA_W_PAD = 128
LORA_A_PAD = 128
LORA_G_PAD = 256
RW_COLS_PAD = 3 * BRANCH_WIDTH + LORA_W_PAD + LORA_A_PAD + LORA_G_PAD
CONV_WIDTH = 3
NORM_EPS = 1e-6
GN_EPS = 64e-5
LOG2E = 1.4426950408889634
CHUNK = 64
VMEM_LIMIT = 56 * 1024 * 1024


def _dot(a, b):
    return jnp.dot(a, b, preferred_element_type=F32)


def _dot_t(a, b):
    return lax.dot_general(a, b, (((1,), (1,)), ((), ())), preferred_element_type=F32)


def _split(a):
    hi = a.astype(BF16)
    lo = (a - hi.astype(F32)).astype(BF16)
    return hi, lo


def _dot3(a, b):
    ah, al = _split(a)
    bh, bl = _split(b)
    return _dot(ah, bh) + _dot(ah, bl) + _dot(al, bh)


def _dot1(a, b):
    return _dot(a.astype(BF16), b.astype(BF16))


def _dot2l(a, b_bf16):
    ah, al = _split(a)
    return _dot(ah, b_bf16) + _dot(al, b_bf16)


def _dot2r(a_bf16, b):
    bh, bl = _split(b)
    return _dot(a_bf16, bh) + _dot(a_bf16, bl)


def _sigmoid(x):
    return 1.0 / (1.0 + jnp.exp(-x))


def _softplus(x):
    return jnp.maximum(x, 0.0) + jnp.log(1.0 + jnp.exp(-jnp.abs(x)))


def _inproj_kernel(x_ref, g_ref, wsb_ref, wrw_ref, wgate_ref, mu_ref,
                   psb_ref, xs_ref, pgate_ref, carry_ref, *, tiles_per_seq):
    i = pl.program_id(0)
    x = x_ref[...]
    h = x * lax.rsqrt(jnp.mean(x * x, axis=-1, keepdims=True) + NORM_EPS) * g_ref[...]
    hb = h.astype(BF16)
    psb_ref[...] = _dot(hb, wsb_ref[...])
    pgate_ref[...] = _dot(hb, wgate_ref[...])
    p = _dot(hb, wrw_ref[...])
    first = (i % tiles_per_seq) == 0
    prev_last = jnp.where(first, 0.0, carry_ref[7:8, :])
    rolled = pltpu.roll(p, 1, 0)
    row = lax.broadcasted_iota(jnp.int32, p.shape, 0)
    prev = jnp.where(row == 0, prev_last, rolled)
    xs_ref[...] = p + (prev - p) * mu_ref[...]
    carry_ref[...] = p[p.shape[0] - 8:, :]


def _in_projection(x2, g, w_sb, w_rw, w_gate, mu, seq, tm):
    t, d = x2.shape
    n_sb, n_rw, n_gate = w_sb.shape[1], w_rw.shape[1], w_gate.shape[1]
    const = lambda i: (0, 0)
    row = lambda i: (i, 0)
    return pl.pallas_call(
        functools.partial(_inproj_kernel, tiles_per_seq=seq // tm),
        grid=(t // tm,),
        in_specs=[
            pl.BlockSpec((tm, d), row),
            pl.BlockSpec((1, d), const),
            pl.BlockSpec((d, n_sb), const),
            pl.BlockSpec((d, n_rw), const),
            pl.BlockSpec((d, n_gate), const),
            pl.BlockSpec((1, n_rw), const),
        ],
        out_specs=[
            pl.BlockSpec((tm, n_sb), row),
            pl.BlockSpec((tm, n_rw), row),
            pl.BlockSpec((tm, n_gate), row),
        ],
        out_shape=[
            jax.ShapeDtypeStruct((t, n_sb), F32),
            jax.ShapeDtypeStruct((t, n_rw), F32),
            jax.ShapeDtypeStruct((t, n_gate), F32),
        ],
        scratch_shapes=[pltpu.VMEM((8, n_rw), F32)],
        compiler_params=pltpu.CompilerParams(
            dimension_semantics=("arbitrary",), vmem_limit_bytes=VMEM_LIMIT),
        name="in_projection",
    )(x2, g, w_sb, w_rw, w_gate, mu)


def _head_rmsnorm(x, g_pair, lane_lo):
    sq = x * x
    s_lo = jnp.sum(jnp.where(lane_lo, sq, 0.0), axis=-1, keepdims=True)
    s_all = jnp.sum(sq, axis=-1, keepdims=True)
    ms = jnp.where(lane_lo, s_lo, s_all - s_lo) * (1.0 / HEAD_DIM)
    return x * lax.rsqrt(ms + NORM_EPS) * g_pair


def _sb_attn_kernel(q_ref, k_ref, v_ref, qg_ref, kg_ref, tri_ref, o_ref,
                    kn_ref, vn_ref, *, blk, row_split):
    qi = pl.program_id(2)
    lane_lo_k = lax.broadcasted_iota(jnp.int32, k_ref.shape, 1) < HEAD_DIM

    @pl.when(qi == 0)
    def _():
        kn = _head_rmsnorm(k_ref[...], kg_ref[...], lane_lo_k).astype(BF16)
        vb = v_ref[...].astype(BF16)
        for hh in range(HEADS_PER_GROUP):
            kn_ref[hh] = kn[:, hh * HEAD_DIM:(hh + 1) * HEAD_DIM]
            vn_ref[hh] = vb[:, hh * HEAD_DIM:(hh + 1) * HEAD_DIM]

    lane_lo_q = lax.broadcasted_iota(jnp.int32, q_ref.shape, 1) < HEAD_DIM
    qn = (_head_rmsnorm(q_ref[...], qg_ref[...], lane_lo_q)
          * (HEAD_DIM ** -0.5 * LOG2E)).astype(BF16)
    tri = tri_ref[...]
    t_idx = lax.broadcasted_iota(jnp.int32, (blk, blk), 0)
    s_idx = lax.broadcasted_iota(jnp.int32, (blk, blk), 1)
    strict = s_idx < t_idx

    rows = blk // row_split
    chains = [(hh, rp) for hh in range(HEADS_PER_GROUP) for rp in range(row_split)]
    q_parts = [qn[rp * rows:(rp + 1) * rows, hh * HEAD_DIM:(hh + 1) * HEAD_DIM]
               for hh, rp in chains]
    strict_parts = [strict[rp * rows:(rp + 1) * rows, :] for _, rp in chains]

    n = len(chains)
    sign_bit = jnp.uint32(0x80000000)

    def scores(start):
        kbs = [kn_ref[hh, pl.ds(start, blk), :] for hh in range(HEADS_PER_GROUP)]
        return [_dot_t(q_parts[ci], kbs[chains[ci][0]]) for ci in range(n)]

    def sweep(start, zs, state, diag):
        vbs = [vn_ref[hh, pl.ds(start, blk), :] for hh in range(HEADS_PER_GROUP)]
        ls_pos, lks = [], []
        for ci in range(n):
            z = zs[ci]
            neg_abs = lax.bitcast_convert_type(
                lax.bitcast_convert_type(z, jnp.uint32) | sign_bit, F32)
            sp = jnp.log(1.0 + jnp.exp2(neg_abs)) * LOG2E
            ls = jnp.minimum(z, 0.0) - sp
            lk = ls - z
            if diag:
                lk = jnp.where(strict_parts[ci], lk, 0.0)
            ls_pos.append(ls)
            lks.append(lk)
        afters = [_dot1(lks[ci], tri) for ci in range(n)]
        new = []
        for ci in range(n):
            carry, acc = state[ci]
            w = jnp.exp2(ls_pos[ci] + afters[ci])
            if diag:
                w = jnp.where(strict_parts[ci], w, 0.0)
            acc = acc + jnp.exp2(carry) * _dot(w.astype(BF16), vbs[chains[ci][0]])
            carry = carry + jnp.sum(lks[ci], axis=-1, keepdims=True)
            new.append((carry, acc))
        return new

    state = [(jnp.zeros((rows, 1), F32), jnp.zeros((rows, HEAD_DIM), F32)) for _ in chains]
    zs_diag = scores(pl.multiple_of(qi * blk, blk))
    zs_next = scores(pl.multiple_of(jnp.maximum(qi - 1, 0) * blk, blk))
    state = sweep(pl.multiple_of(qi * blk, blk), zs_diag, state, True)

    def body(j, carried):
        st, zs = carried
        zs_following = scores(pl.multiple_of(jnp.maximum(qi - 2 - j, 0) * blk, blk))
        st = sweep(pl.multiple_of((qi - 1 - j) * blk, blk), zs, st, False)
        return st, zs_following

    state, _ = lax.fori_loop(0, qi, body, (state, zs_next))
    o_ref[...] = jnp.concatenate(
        [jnp.concatenate([state[hh * row_split + rp][1] for rp in range(row_split)], axis=0)
         for hh in range(HEADS_PER_GROUP)], axis=-1)


def _sb_attention(p_sb3, qg_pair, kg_pair, tri, blk, row_split):
    bsz, seq, _ = p_sb3.shape
    nq = seq // blk
    kblk0 = BRANCH_WIDTH // LANES
    return pl.pallas_call(
        functools.partial(_sb_attn_kernel, blk=blk, row_split=row_split),
        grid=(bsz, N_GROUPS, nq),
        in_specs=[
            pl.BlockSpec((None, blk, LANES), lambda b, g, i: (b, i, g)),
            pl.BlockSpec((None, seq, LANES), lambda b, g, i: (b, 0, kblk0 + g)),
            pl.BlockSpec((None, seq, LANES), lambda b, g, i: (b, 0, 2 * kblk0 + g)),
            pl.BlockSpec((1, LANES), lambda b, g, i: (0, 0)),
            pl.BlockSpec((1, LANES), lambda b, g, i: (0, 0)),
            pl.BlockSpec((blk, blk), lambda b, g, i: (0, 0)),
        ],
        out_specs=pl.BlockSpec((None, blk, LANES), lambda b, g, i: (b, i, g)),
        out_shape=jax.ShapeDtypeStruct((bsz, seq, BRANCH_WIDTH), F32),
        scratch_shapes=[
            pltpu.VMEM((HEADS_PER_GROUP, seq, HEAD_DIM), BF16),
            pltpu.VMEM((HEADS_PER_GROUP, seq, HEAD_DIM), BF16),
        ],
        compiler_params=pltpu.CompilerParams(
            dimension_semantics=("parallel", "parallel", "arbitrary"),
            vmem_limit_bytes=VMEM_LIMIT),
        name="sb_attention",
    )(p_sb3, p_sb3, p_sb3, qg_pair, kg_pair, tri)


def _rwkv_kernel(xs_ref, dbase_ref, dup_ref, ibase_ref, iup_ref, gup_ref, keyk_ref,
                 keya_ref, bonus_ref, gnw_ref, gnb_ref, seg_ref, tri_ref, o_ref,
                 h_ref, *, n_sub):
    c = CHUNK
    bw = BRANCH_WIDTH

    @pl.when(pl.program_id(1) == 0)
    def _():
        h_ref[...] = jnp.zeros_like(h_ref)

    xs = xs_ref[...]
    r = xs[:, 0:bw]
    k = xs[:, bw:2 * bw]
    v = xs[:, 2 * bw:3 * bw]
    o0 = 3 * bw
    w_lo = xs[:, o0:o0 + LORA_W_PAD]
    a_lo = xs[:, o0 + LORA_W_PAD:o0 + LORA_W_PAD + LORA_A_PAD]
    g_lo = xs[:, o0 + LORA_W_PAD + LORA_A_PAD:]
    seg = seg_ref[...]

    dw = dbase_ref[...] + _dot3(jnp.tanh(w_lo), dup_ref[...])
    w_log = -_softplus(-dw) - 0.5
    lw = -jnp.exp(w_log)
    iclr = _sigmoid(ibase_ref[...] + _dot1(a_lo, iup_ref[...]))
    gate = _dot1(_sigmoid(g_lo), gup_ref[...])
    kk0 = k * keyk_ref[...]
    kk = kk0 / jnp.maximum(jnp.sqrt(_dot1(kk0 * kk0, seg)), 1e-12)
    kmod = k * (1.0 + (iclr - 1.0) * keya_ref[...])
    a = -kk
    b = kk * iclr
    bonus = _dot1(r * kmod * bonus_ref[...], seg) * v

    tri = tri_ref[...]
    row = lax.broadcasted_iota(jnp.int32, (LANES, LANES), 0)
    col = lax.broadcasted_iota(jnp.int32, (LANES, LANES), 1)
    same_head = (row // c) == (col // c)
    strict_lo = jnp.logical_and(same_head, (col % c) < (row % c))
    incl_lo = jnp.logical_and(same_head, (col % c) <= (row % c))
    eye = row == col
    lane = lax.broadcasted_iota(jnp.int32, (c, LANES), 1)
    head_masks = [(lane // HEAD_DIM) == hh for hh in range(HEADS_PER_GROUP)]

    def stack(x):
        return jnp.concatenate([jnp.where(m, x, 0.0) for m in head_masks], axis=0)

    chains = [(s, g) for s in range(n_sub) for g in range(N_GROUPS)]
    ops = {}
    for s in range(n_sub):
        sl = slice(s * c, (s + 1) * c)
        lw_s = lw[sl]
        cum = _dot2r(tri, lw_s)
        tot = cum[c - 1:c, :]
        e_cum = jnp.exp(cum)
        e_prev = jnp.exp(cum - lw_s)
        e_inv = jnp.exp(-cum)
        e_rem = jnp.exp(tot - cum)
        p_tot = jnp.exp(tot)
        full = dict(a=a[sl] * e_prev, r=r[sl] * e_cum, b=b[sl] * e_inv, k=kmod[sl] * e_inv,
                    bh=b[sl] * e_rem, kh=kmod[sl] * e_rem, v=v[sl])
        for g in range(N_GROUPS):
            ls = slice(g * LANES, (g + 1) * LANES)
            st = {name: stack(val[:, ls]) for name, val in full.items()}
            ops[s, g] = dict(
                a=st["a"].astype(BF16), b=st["b"].astype(BF16), k=st["k"].astype(BF16),
                v=st["v"].astype(BF16), r=st["r"], bh_t=st["bh"].T.astype(BF16),
                kh_t=st["kh"].T.astype(BF16), p_tot=p_tot[:, ls])

    l_pow, t_inv, l_ak, m_rb, m_rk = {}, {}, {}, {}, {}
    for ch in chains:
        o = ops[ch]
        sc = _dot_t(jnp.concatenate([o["a"], o["r"].astype(BF16)], axis=0),
                    jnp.concatenate([o["b"], o["k"]], axis=0))
        l_ab = jnp.where(strict_lo, sc[:2 * c, :2 * c], 0.0)
        l_ak[ch] = jnp.where(strict_lo, sc[:2 * c, 2 * c:], 0.0).astype(BF16)
        m_rb[ch] = jnp.where(incl_lo, sc[2 * c:, :2 * c], 0.0).astype(BF16)
        m_rk[ch] = jnp.where(incl_lo, sc[2 * c:, 2 * c:], 0.0).astype(BF16)
        l_pow[ch] = l_ab
        t_inv[ch] = jnp.where(eye, 1.0, l_ab)
    for _ in range(5):
        for ch in chains:
            lp = l_pow[ch].astype(BF16)
            l_pow[ch] = _dot(lp, lp)
        for ch in chains:
            t_inv[ch] = t_inv[ch] + _dot1(t_inv[ch], l_pow[ch])
    rh2, yl2, m_mat, n_mat = {}, {}, {}, {}
    x_loc = {ch: _dot(l_ak[ch], ops[ch]["v"]) for ch in chains}
    zw = {ch: _dot(t_inv[ch].astype(BF16),
                   jnp.concatenate([ops[ch]["a"], x_loc[ch].astype(BF16)], axis=1))
          for ch in chains}
    for ch in chains:
        o = ops[ch]
        ap2 = zw[ch][:, :LANES].astype(BF16)
        wt2 = zw[ch][:, LANES:].astype(BF16)
        rh2[ch] = o["r"] + _dot(m_rb[ch], ap2)
        yl2[ch] = _dot(m_rb[ch], wt2) + _dot(m_rk[ch], o["v"])
        m_mat[ch] = _dot(o["bh_t"], ap2) + jnp.where(eye, o["p_tot"], 0.0)
        n_mat[ch] = _dot(o["bh_t"], wt2) + _dot(o["kh_t"], o["v"])

    ys = []
    h = [h_ref[g] for g in range(N_GROUPS)]
    for s in range(n_sub):
        y_groups = []
        for g in range(N_GROUPS):
            y2 = _dot1(rh2[s, g], h[g]) + yl2[s, g]
            h[g] = _dot3(m_mat[s, g], h[g]) + n_mat[s, g]
            y_groups.append(y2[:c] + y2[c:])
        ys.append(jnp.concatenate(y_groups, axis=-1))
    for g in range(N_GROUPS):
        h_ref[g] = h[g]
    y = jnp.concatenate(ys, axis=0) if n_sub > 1 else ys[0]

    inv_n = 1.0 / HEAD_DIM
    mu = _dot2l(y, seg) * inv_n
    d = y - mu
    var = _dot1(d * d, seg) * inv_n
    yn = d * lax.rsqrt(var + GN_EPS) * gnw_ref[...] + gnb_ref[...]
    o_ref[...] = (yn + bonus) * gate


def _rwkv_mix(xs3, dbase, dup, ibase, iup, gup, keyk, keya, bonus, gnw, gnb, seg, tri, tc):
    bsz, seq, ncol = xs3.shape
    const = lambda b, i: (0, 0)
    vec = pl.BlockSpec((1, BRANCH_WIDTH), const)
    return pl.pallas_call(
        functools.partial(_rwkv_kernel, n_sub=tc // CHUNK),
        grid=(bsz, seq // tc),
        in_specs=[
            pl.BlockSpec((None, tc, ncol), lambda b, i: (b, i, 0)),
            vec,
            pl.BlockSpec(dup.shape, const),
            vec,
            pl.BlockSpec(iup.shape, const),
            pl.BlockSpec(gup.shape, const),
            vec, vec, vec, vec, vec,
            pl.BlockSpec(seg.shape, const),
            pl.BlockSpec(tri.shape, const),
        ],
        out_specs=pl.BlockSpec((None, tc, BRANCH_WIDTH), lambda b, i: (b, i, 0)),
        out_shape=jax.ShapeDtypeStruct((bsz, seq, BRANCH_WIDTH), F32),
        scratch_shapes=[pltpu.VMEM((N_GROUPS, LANES, LANES), F32)],
        compiler_params=pltpu.CompilerParams(
            dimension_semantics=("parallel", "arbitrary"), vmem_limit_bytes=VMEM_LIMIT),
        name="rwkv_mix",
    )(xs3, dbase, dup, ibase, iup, gup, keyk, keya, bonus, gnw, gnb, seg, tri)


def _merge_kernel(x_ref, osb_ref, orw_ref, pg_ref, bg_ref, wb0_ref, wb1_ref, wout_ref,
                  g2_ref, x1_ref, h2_ref):
    d = x_ref.shape[1]
    up0 = _dot(osb_ref[...].astype(BF16), wb0_ref[...])
    up1 = _dot(orw_ref[...].astype(BF16), wb1_ref[...])
    gates = _sigmoid(pg_ref[...] + bg_ref[...])
    mixed = gates[:, :d] * up0 + gates[:, d:] * up1
    x1 = x_ref[...] + _dot(mixed.astype(BF16), wout_ref[...])
    x1_ref[...] = x1
    h2 = x1 * lax.rsqrt(jnp.mean(x1 * x1, axis=-1, keepdims=True) + NORM_EPS) * g2_ref[...]
    h2_ref[...] = h2.astype(BF16)


def _merge(x2, o_sb, o_rw, p_gate, bg, wb0, wb1, w_out, g2, tm):
    t, d = x2.shape
    const = lambda i: (0, 0)
    row = lambda i: (i, 0)
    return pl.pallas_call(
        _merge_kernel,
        grid=(t // tm,),
        in_specs=[
            pl.BlockSpec((tm, d), row),
            pl.BlockSpec((tm, BRANCH_WIDTH), row),
            pl.BlockSpec((tm, BRANCH_WIDTH), row),
            pl.BlockSpec((tm, 2 * d), row),
            pl.BlockSpec((1, 2 * d), const),
            pl.BlockSpec(wb0.shape, const),
            pl.BlockSpec(wb1.shape, const),
            pl.BlockSpec(w_out.shape, const),
            pl.BlockSpec((1, d), const),
        ],
        out_specs=[pl.BlockSpec((tm, d), row), pl.BlockSpec((tm, d), row)],
        out_shape=[jax.ShapeDtypeStruct((t, d), F32), jax.ShapeDtypeStruct((t, d), BF16)],
        compiler_params=pltpu.CompilerParams(
            dimension_semantics=("parallel",), vmem_limit_bytes=VMEM_LIMIT),
        name="merge_outproj",
    )(x2, o_sb, o_rw, p_gate, bg, wb0, wb1, w_out, g2)


def _ffn_kernel(h2_ref, x1_ref, wup_ref, cw_ref, cb_ref, wdown_ref, o_ref, carry_ref,
                *, tiles_per_seq, d_ff, nc):
    i = pl.program_id(0)
    first = (i % tiles_per_seq) == 0
    h2 = h2_ref[...]
    tm = h2.shape[0]
    row = lax.broadcasted_iota(jnp.int32, (tm, nc), 0)

    def conv(u, col0):
        cs = slice(col0, col0 + nc)
        old = jnp.where(first, 0.0, carry_ref[:, cs])
        p1 = jnp.where(row == 0, old[7:8, :], pltpu.roll(u, 1, 0))
        p2 = jnp.where(row == 0, old[6:7, :],
                       jnp.where(row == 1, old[7:8, :], pltpu.roll(u, 2, 0)))
        carry_ref[:, cs] = u[tm - 8:, :]
        cw = cw_ref[:, cs]
        return cw[0:1] * p2 + cw[1:2] * p1 + cw[2:3] * u + cb_ref[:, cs]

    acc = x1_ref[...]
    for ch in range(d_ff // nc):
        cg = conv(_dot(h2, wup_ref[:, ch * nc:(ch + 1) * nc]), ch * nc)
        cv = conv(_dot(h2, wup_ref[:, d_ff + ch * nc:d_ff + (ch + 1) * nc]), d_ff + ch * nc)
        act = cg * _sigmoid(cg) * cv
        acc = acc + _dot(act.astype(BF16), wdown_ref[ch * nc:(ch + 1) * nc, :])
    o_ref[...] = acc


def _ffn(h2, x1, w_up, conv_w, conv_b, w_down, seq, tm, nc):
    t, d = x1.shape
    d_ff = w_down.shape[0]
    const = lambda i: (0, 0)
    row = lambda i: (i, 0)
    return pl.pallas_call(
        functools.partial(_ffn_kernel, tiles_per_seq=seq // tm, d_ff=d_ff, nc=nc),
        grid=(t // tm,),
        in_specs=[
            pl.BlockSpec((tm, d), row),
            pl.BlockSpec((tm, d), row),
            pl.BlockSpec(w_up.shape, const),
            pl.BlockSpec(conv_w.shape, const),
            pl.BlockSpec(conv_b.shape, const),
            pl.BlockSpec(w_down.shape, const),
        ],
        out_specs=pl.BlockSpec((tm, d), row),
        out_shape=jax.ShapeDtypeStruct((t, d), F32),
        scratch_shapes=[pltpu.VMEM((8, 2 * d_ff), F32)],
        compiler_params=pltpu.CompilerParams(
            dimension_semantics=("arbitrary",), vmem_limit_bytes=VMEM_LIMIT),
        name="ffn",
    )(h2, x1, w_up, conv_w, conv_b, w_down)


def _pad_cols(w, n):
    return jnp.pad(w, ((0, 0), (0, n - w.shape[1])))


def _pad_rows(w, n):
    return jnp.pad(w, ((0, n - w.shape[0]), (0, 0)))


def _layer(x, attn_norm_g, w_in, q_norm_g, k_norm_g, rwkv_shift_mu, decay_base, decay_up,
           iclr_base, iclr_up, out_gate_up, key_k, key_a, bonus_rk, group_norm_w,
           group_norm_b, branch_gate_b, w_branch, w_out, ffn_norm_g, w_ffn_up,
           ffn_conv_w, ffn_conv_b, w_ffn_down):
    bsz, seq, d = x.shape
    t = bsz * seq
    bw = BRANCH_WIDTH
    sb_cols = 3 * bw
    rw_cols = 3 * bw + DECAY_LORA + ICLR_LORA + GATE_LORA
    x2 = x.reshape(t, d)

    w_sb = w_in[:, :sb_cols].astype(BF16)
    w_rw_raw = w_in[:, sb_cols:sb_cols + rw_cols]
    mu_raw = rwkv_shift_mu.reshape(1, rw_cols)

    def regroup(m):
        o = 3 * bw
        return jnp.concatenate([
            m[:, :o],
            _pad_cols(m[:, o:o + DECAY_LORA], LORA_W_PAD),
            _pad_cols(m[:, o + DECAY_LORA:o + DECAY_LORA + ICLR_LORA], LORA_A_PAD),
            _pad_cols(m[:, o + DECAY_LORA + ICLR_LORA:], LORA_G_PAD),
        ], axis=1)

    w_rw = regroup(w_rw_raw).astype(BF16)
    mu = regroup(mu_raw)
    w_gate = w_in[:, sb_cols + rw_cols:].astype(BF16)

    p_sb, xs, p_gate = _in_projection(
        x2, attn_norm_g.reshape(1, d), w_sb, w_rw, w_gate, mu, seq, tm=256)

    blk = 256
    tri_sb = (jnp.arange(blk)[:, None] > jnp.arange(blk)[None, :]).astype(BF16)
    qg_pair = jnp.tile(q_norm_g.reshape(1, HEAD_DIM), (1, HEADS_PER_GROUP))
    kg_pair = jnp.tile(k_norm_g.reshape(1, HEAD_DIM), (1, HEADS_PER_GROUP))
    o_sb = _sb_attention(p_sb.reshape(bsz, seq, sb_cols), qg_pair, kg_pair, tri_sb, blk,
                         row_split=2)

    head_id = jnp.arange(bw) // HEAD_DIM
    seg = (head_id[:, None] == head_id[None, :]).astype(BF16)
    tri_rw = (jnp.arange(CHUNK)[:, None] >= jnp.arange(CHUNK)[None, :]).astype(BF16)
    vec = lambda p: p.reshape(1, bw)
    o_rw = _rwkv_mix(
        xs.reshape(bsz, seq, RW_COLS_PAD), vec(decay_base), _pad_rows(decay_up, LORA_W_PAD),
        vec(iclr_base), _pad_rows(iclr_up, LORA_A_PAD), _pad_rows(out_gate_up, LORA_G_PAD),
        vec(key_k), vec(key_a), vec(bonus_rk), vec(group_norm_w), vec(group_norm_b),
        seg, tri_rw, tc=128)

    x1, h2 = _merge(
        x2, o_sb.reshape(t, bw), o_rw.reshape(t, bw), p_gate,
        branch_gate_b.reshape(1, 2 * d), w_branch[0].astype(BF16), w_branch[1].astype(BF16),
        w_out.astype(BF16), ffn_norm_g.reshape(1, d), tm=512)

    out = _ffn(h2, x1, w_ffn_up.astype(BF16), ffn_conv_w, ffn_conv_b.reshape(1, -1),
               w_ffn_down.astype(BF16), seq, tm=256, nc=256)
    return out.reshape(bsz, seq, d)


def kernel(x, attn_norm_g, w_in, q_norm_g, k_norm_g, rwkv_shift_mu, decay_base, decay_up,
           iclr_base, iclr_up, out_gate_up, key_k, key_a, bonus_rk, group_norm_w,
           group_norm_b, branch_gate_b, w_branch, w_out, ffn_norm_g, w_ffn_up, ffn_conv_w,
           ffn_conv_b, w_ffn_down):
    for layer in range(attn_norm_g.shape[0]):
        x = _layer(
            x, attn_norm_g[layer], w_in[layer], q_norm_g[layer], k_norm_g[layer],
            rwkv_shift_mu[layer], decay_base[layer], decay_up[layer], iclr_base[layer],
            iclr_up[layer], out_gate_up[layer], key_k[layer], key_a[layer], bonus_rk[layer],
            group_norm_w[layer], group_norm_b[layer], branch_gate_b[layer], w_branch[layer],
            w_out[layer], ffn_norm_g[layer], w_ffn_up[layer], ffn_conv_w[layer],
            ffn_conv_b[layer], w_ffn_down[layer])
    return x
```

```python
import functools

import jax
import jax.numpy as jnp
from jax import lax
from jax.experimental import pallas as pl
from jax.experimental.pallas import tpu as pltpu

F32 = jnp.float32
BF16 = jnp.bfloat16

HEAD_DIM = 64
N_HEADS = 8
BRANCH_WIDTH = N_HEADS * HEAD_DIM
LANES = 128
HEADS_PER_GROUP = LANES // HEAD_DIM
N_GROUPS = BRANCH_WIDTH // LANES
DECAY_LORA = 64
ICLR_LORA = 64
GATE_LORA = 160
LORA_W_PAD = 128
LORA_A_PAD = 128
LORA_G_PAD = 256
RW_COLS_PAD = 3 * BRANCH_WIDTH + LORA_W_PAD + LORA_A_PAD + LORA_G_PAD
CONV_WIDTH = 3
NORM_EPS = 1e-6
GN_EPS = 64e-5
LOG2E = 1.4426950408889634
MASKED_LOG2 = -1e30
CHUNK = 64
VMEM_LIMIT = 56 * 1024 * 1024


def _dot(a, b):
    return jnp.dot(a, b, preferred_element_type=F32)


def _dot_t(a, b):
    return lax.dot_general(a, b, (((1,), (1,)), ((), ())), preferred_element_type=F32)


def _split(a):
    hi = a.astype(BF16)
    lo = (a - hi.astype(F32)).astype(BF16)
    return hi, lo


def _dot3(a, b):
    ah, al = _split(a)
    bh, bl = _split(b)
    return _dot(ah, bh) + _dot(ah, bl) + _dot(al, bh)


def _dot1(a, b):
    return _dot(a.astype(BF16), b.astype(BF16))


def _dot2l(a, b_bf16):
    ah, al = _split(a)
    return _dot(ah, b_bf16) + _dot(al, b_bf16)


def _dot2r(a_bf16, b):
    bh, bl = _split(b)
    return _dot(a_bf16, bh) + _dot(a_bf16, bl)


def _sigmoid(x):
    return 1.0 / (1.0 + jnp.exp(-x))


def _softplus(x):
    return jnp.maximum(x, 0.0) + jnp.log(1.0 + jnp.exp(-jnp.abs(x)))


def _inproj_kernel(x_ref, g_ref, wsb_ref, wrw_ref, wgate_ref, mu_ref,
                   psb_ref, xs_ref, pgate_ref, carry_ref, *, tiles_per_seq):
    i = pl.program_id(0)
    x = x_ref[...]
    h = x * lax.rsqrt(jnp.mean(x * x, axis=-1, keepdims=True) + NORM_EPS) * g_ref[...]
    hb = h.astype(BF16)
    psb_ref[...] = _dot(hb, wsb_ref[...])
    pgate_ref[...] = _dot(hb, wgate_ref[...]).astype(BF16)
    p = _dot(hb, wrw_ref[...])
    first = (i % tiles_per_seq) == 0
    prev_last = jnp.where(first, 0.0, carry_ref[7:8, :])
    rolled = pltpu.roll(p, 1, 0)
    row = lax.broadcasted_iota(jnp.int32, p.shape, 0)
    prev = jnp.where(row == 0, prev_last, rolled)
    xs_ref[...] = p + (prev - p) * mu_ref[...]
    carry_ref[...] = p[p.shape[0] - 8:, :]


def _in_projection(x2, g, w_sb, w_rw, w_gate, mu, seq, tm):
    t, d = x2.shape
    n_sb, n_rw, n_gate = w_sb.shape[1], w_rw.shape[1], w_gate.shape[1]
    const = lambda i: (0, 0)
    row = lambda i: (i, 0)
    return pl.pallas_call(
        functools.partial(_inproj_kernel, tiles_per_seq=seq // tm),
        grid=(t // tm,),
        in_specs=[
            pl.BlockSpec((tm, d), row),
            pl.BlockSpec((1, d), const),
            pl.BlockSpec((d, n_sb), const),
            pl.BlockSpec((d, n_rw), const),
            pl.BlockSpec((d, n_gate), const),
            pl.BlockSpec((1, n_rw), const),
        ],
        out_specs=[
            pl.BlockSpec((tm, n_sb), row),
            pl.BlockSpec((tm, n_rw), row),
            pl.BlockSpec((tm, n_gate), row),
        ],
        out_shape=[
            jax.ShapeDtypeStruct((t, n_sb), F32),
            jax.ShapeDtypeStruct((t, n_rw), F32),
            jax.ShapeDtypeStruct((t, n_gate), BF16),
        ],
        scratch_shapes=[pltpu.VMEM((8, n_rw), F32)],
        compiler_params=pltpu.CompilerParams(
            dimension_semantics=("arbitrary",), vmem_limit_bytes=VMEM_LIMIT),
        name="in_projection",
    )(x2, g, w_sb, w_rw, w_gate, mu)


def _head_rmsnorm(x, g_pair, lane_lo):
    sq = x * x
    s_lo = jnp.sum(jnp.where(lane_lo, sq, 0.0), axis=-1, keepdims=True)
    s_all = jnp.sum(sq, axis=-1, keepdims=True)
    ms = jnp.where(lane_lo, s_lo, s_all - s_lo) * (1.0 / HEAD_DIM)
    return x * lax.rsqrt(ms + NORM_EPS) * g_pair


def _sb_attn_kernel(q_ref, k_ref, v_ref, qg_ref, kg_ref, tri_ref, o_ref,
                    kn_ref, vm_ref, z_ref, x_ref, keep_ref, acc_ref, carry_ref, *, blk):
    qi = pl.program_id(2)
    lane_lo_k = lax.broadcasted_iota(jnp.int32, k_ref.shape, 1) < HEAD_DIM

    @pl.when(qi == 0)
    def _():
        kn_ref[...] = _head_rmsnorm(k_ref[...], kg_ref[...], lane_lo_k).astype(BF16)
        v = v_ref[...]
        vm_ref[0] = jnp.where(lane_lo_k, v, 0.0).astype(BF16)
        vm_ref[1] = jnp.where(lane_lo_k, 0.0, v).astype(BF16)

    lane_lo = lax.broadcasted_iota(jnp.int32, q_ref.shape, 1) < HEAD_DIM
    qn = _head_rmsnorm(q_ref[...], qg_ref[...], lane_lo) * (HEAD_DIM ** -0.5 * LOG2E)
    q_heads = [jnp.where(lane_lo, qn, 0.0).astype(BF16), jnp.where(lane_lo, 0.0, qn).astype(BF16)]
    tri = tri_ref[...]
    t_idx = lax.broadcasted_iota(jnp.int32, (blk, blk), 0)
    s_idx = lax.broadcasted_iota(jnp.int32, (blk, blk), 1)
    strict = s_idx < t_idx

    sign_bit = jnp.uint32(0x80000000)
    heads = range(HEADS_PER_GROUP)
    lane_lo_half = lax.broadcasted_iota(jnp.int32, (blk // 2, LANES), 1) < HEAD_DIM

    def key_start(i):
        return pl.multiple_of(jnp.maximum(qi - i, 0) * blk, blk)

    def scores(start):
        kb = kn_ref[pl.ds(start, blk), :]
        return [_dot_t(q_heads[hh], kb) for hh in heads]

    def front(zs, slot_w, diag):
        half = blk // 2
        for rows in (slice(0, half), slice(half, blk)):
            lss, lks = [], []
            for z in zs:
                z = z[rows]
                neg_abs = lax.bitcast_convert_type(
                    lax.bitcast_convert_type(z, jnp.uint32) | sign_bit, F32)
                sp = jnp.log(1.0 + jnp.exp2(neg_abs)) * LOG2E
                ls = jnp.minimum(z, 0.0) - sp
                lk = ls - z
                lss.append(ls)
                lks.append(jnp.where(strict[rows], lk, 0.0) if diag else lk)
            afters = [_dot1(lk, tri) for lk in lks]
            sums = [jnp.sum(lk, axis=-1, keepdims=True) for lk in lks]
            sum_pair = jnp.where(lane_lo_half, sums[0], sums[1])
            if diag:
                keep_ref[slot_w, rows, :] = jnp.ones((half, LANES), F32)
                carry_ref[rows, :] = sum_pair
            else:
                carry = carry_ref[rows, :]
                keep_ref[slot_w, rows, :] = jnp.exp2(carry)
                carry_ref[rows, :] = carry + sum_pair
            for hh in heads:
                x = lss[hh] + afters[hh]
                x_ref[slot_w, hh, rows, :] = (
                    jnp.where(strict[rows], x, MASKED_LOG2) if diag else x)

    def back(start, slot_r):
        ws = [jnp.exp2(x_ref[slot_r, hh]).astype(BF16) for hh in heads]
        pv = sum(_dot(ws[hh], vm_ref[hh, pl.ds(start, blk), :]) for hh in heads)
        acc_ref[...] = acc_ref[...] + keep_ref[slot_r] * pv

    def step(i, slot):
        zs = [z_ref[slot, hh] for hh in heads]
        nxt = scores(key_start(i + 1))
        for hh in heads:
            z_ref[1 - slot, hh] = nxt[hh]
        back(key_start(i - 1), 1 - slot)
        front(zs, slot, False)

    acc_ref[...] = jnp.zeros_like(acc_ref)
    front(scores(key_start(0)), 0, True)
    nxt = scores(key_start(1))
    for hh in heads:
        z_ref[1, hh] = nxt[hh]

    def body(t, _):
        step(2 * t + 1, 1)

        @pl.when(2 * t + 2 <= qi)
        def _():
            step(2 * t + 2, 0)

        return 0

    lax.fori_loop(0, (qi + 1) // 2, body, 0)
    back(key_start(qi), qi % 2)
    o_ref[...] = acc_ref[...].astype(BF16)


def _sb_attention(p_sb3, qg_pair, kg_pair, tri, blk):
    bsz, seq, _ = p_sb3.shape
    nq = seq // blk
    kblk0 = BRANCH_WIDTH // LANES
    return pl.pallas_call(
        functools.partial(_sb_attn_kernel, blk=blk),
        grid=(bsz, N_GROUPS, nq),
        in_specs=[
            pl.BlockSpec((None, blk, LANES), lambda b, g, i: (b, i, g)),
            pl.BlockSpec((None, seq, LANES), lambda b, g, i: (b, 0, kblk0 + g)),
            pl.BlockSpec((None, seq, LANES), lambda b, g, i: (b, 0, 2 * kblk0 + g)),
            pl.BlockSpec((1, LANES), lambda b, g, i: (0, 0)),
            pl.BlockSpec((1, LANES), lambda b, g, i: (0, 0)),
            pl.BlockSpec((blk, blk), lambda b, g, i: (0, 0)),
        ],
        out_specs=pl.BlockSpec((None, blk, LANES), lambda b, g, i: (b, i, g)),
        out_shape=jax.ShapeDtypeStruct((bsz, seq, BRANCH_WIDTH), BF16),
        scratch_shapes=[
            pltpu.VMEM((seq, LANES), BF16),
            pltpu.VMEM((HEADS_PER_GROUP, seq, LANES), BF16),
            pltpu.VMEM((2, HEADS_PER_GROUP, blk, blk), F32),
            pltpu.VMEM((2, HEADS_PER_GROUP, blk, blk), F32),
            pltpu.VMEM((2, blk, LANES), F32),
            pltpu.VMEM((blk, LANES), F32),
            pltpu.VMEM((blk, LANES), F32),
        ],
        compiler_params=pltpu.CompilerParams(
            dimension_semantics=("parallel", "parallel", "arbitrary"),
            vmem_limit_bytes=VMEM_LIMIT),
        name="sb_attention",
    )(p_sb3, p_sb3, p_sb3, qg_pair, kg_pair, tri)


def _rwkv_kernel(xs_ref, dbase_ref, dup_ref, ibase_ref, iup_ref, gup_ref, keyk_ref,
                 keya_ref, bonus_ref, gnw_ref, gnb_ref, seg_ref, tri_ref, o_ref,
                 h_ref, *, n_sub):
    c = CHUNK
    bw = BRANCH_WIDTH

    @pl.when(pl.program_id(1) == 0)
    def _():
        h_ref[...] = jnp.zeros_like(h_ref)

    xs = xs_ref[...]
    r = xs[:, 0:bw]
    k = xs[:, bw:2 * bw]
    v = xs[:, 2 * bw:3 * bw]
    o0 = 3 * bw
    w_lo = xs[:, o0:o0 + LORA_W_PAD]
    a_lo = xs[:, o0 + LORA_W_PAD:o0 + LORA_W_PAD + LORA_A_PAD]
    g_lo = xs[:, o0 + LORA_W_PAD + LORA_A_PAD:]
    seg = seg_ref[...]

    dw = dbase_ref[...] + _dot3(jnp.tanh(w_lo), dup_ref[...])
    w_log = -_softplus(-dw) - 0.5
    lw = -jnp.exp(w_log)
    iclr = _sigmoid(ibase_ref[...] + _dot1(a_lo, iup_ref[...]))
    gate = _dot1(_sigmoid(g_lo), gup_ref[...])
    kk0 = k * keyk_ref[...]
    kk = kk0 / jnp.maximum(jnp.sqrt(_dot1(kk0 * kk0, seg)), 1e-12)
    kmod = k * (1.0 + (iclr - 1.0) * keya_ref[...])
    a = -kk
    b = kk * iclr
    bonus = _dot1(r * kmod * bonus_ref[...], seg) * v

    tri = tri_ref[...]
    row = lax.broadcasted_iota(jnp.int32, (LANES, LANES), 0)
    col = lax.broadcasted_iota(jnp.int32, (LANES, LANES), 1)
    same_head = (row // c) == (col // c)
    strict_lo = jnp.logical_and(same_head, (col % c) < (row % c))
    incl_lo = jnp.logical_and(same_head, (col % c) <= (row % c))
    eye = row == col
    lane = lax.broadcasted_iota(jnp.int32, (c, LANES), 1)
    head_masks = [(lane // HEAD_DIM) == hh for hh in range(HEADS_PER_GROUP)]

    def stack(x):
        return jnp.concatenate([jnp.where(m, x, 0.0) for m in head_masks], axis=0)

    chains = [(s, g) for s in range(n_sub) for g in range(N_GROUPS)]
    ops = {}
    for s in range(n_sub):
        sl = slice(s * c, (s + 1) * c)
        lw_s = lw[sl]
        cum = _dot2r(tri, lw_s)
        tot = cum[c - 1:c, :]
        e_cum = jnp.exp(cum)
        e_prev = jnp.exp(cum - lw_s)
        e_inv = jnp.exp(-cum)
        e_rem = jnp.exp(tot - cum)
        p_tot = jnp.exp(tot)
        full = dict(a=a[sl] * e_prev, r=r[sl] * e_cum, b=b[sl] * e_inv, k=kmod[sl] * e_inv,
                    bh=b[sl] * e_rem, kh=kmod[sl] * e_rem, v=v[sl])
        for g in range(N_GROUPS):
            ls = slice(g * LANES, (g + 1) * LANES)
            st = {name: stack(val[:, ls]) for name, val in full.items()}
            ops[s, g] = dict(
                a=st["a"].astype(BF16), b=st["b"].astype(BF16), k=st["k"].astype(BF16),
                v=st["v"].astype(BF16), r=st["r"], bh_t=st["bh"].T.astype(BF16),
                kh_t=st["kh"].T.astype(BF16), p_tot=p_tot[:, ls])

    l_pow, t_inv, l_ak, m_rb, m_rk = {}, {}, {}, {}, {}
    for ch in chains:
        o = ops[ch]
        sc = _dot_t(jnp.concatenate([o["a"], o["r"].astype(BF16)], axis=0),
                    jnp.concatenate([o["b"], o["k"]], axis=0))
        l_ab = jnp.where(strict_lo, sc[:2 * c, :2 * c], 0.0)
        l_ak[ch] = jnp.where(strict_lo, sc[:2 * c, 2 * c:], 0.0).astype(BF16)
        m_rb[ch] = jnp.where(incl_lo, sc[2 * c:, :2 * c], 0.0).astype(BF16)
        m_rk[ch] = jnp.where(incl_lo, sc[2 * c:, 2 * c:], 0.0).astype(BF16)
        l_pow[ch] = l_ab
        t_inv[ch] = jnp.where(eye, 1.0, l_ab)
    for _ in range(5):
        for ch in chains:
            lp = l_pow[ch].astype(BF16)
            l_pow[ch] = _dot(lp, lp)
        for ch in chains:
            t_inv[ch] = t_inv[ch] + _dot1(t_inv[ch], l_pow[ch])
    rh2, yl2, m_mat, n_mat = {}, {}, {}, {}
    x_loc = {ch: _dot(l_ak[ch], ops[ch]["v"]) for ch in chains}
    zw = {ch: _dot(t_inv[ch].astype(BF16),
                   jnp.concatenate([ops[ch]["a"], x_loc[ch].astype(BF16)], axis=1))
          for ch in chains}
    for ch in chains:
        o = ops[ch]
        ap2 = zw[ch][:, :LANES].astype(BF16)
        wt2 = zw[ch][:, LANES:].astype(BF16)
        rh2[ch] = o["r"] + _dot(m_rb[ch], ap2)
        yl2[ch] = _dot(m_rb[ch], wt2) + _dot(m_rk[ch], o["v"])
        m_mat[ch] = _dot(o["bh_t"], ap2) + jnp.where(eye, o["p_tot"], 0.0)
        n_mat[ch] = _dot(o["bh_t"], wt2) + _dot(o["kh_t"], o["v"])

    ys = []
    h = [h_ref[g] for g in range(N_GROUPS)]
    for s in range(n_sub):
        y_groups = []
        for g in range(N_GROUPS):
            y2 = _dot1(rh2[s, g], h[g]) + yl2[s, g]
            h[g] = _dot3(m_mat[s, g], h[g]) + n_mat[s, g]
            y_groups.append(y2[:c] + y2[c:])
        ys.append(jnp.concatenate(y_groups, axis=-1))
    for g in range(N_GROUPS):
        h_ref[g] = h[g]
    y = jnp.concatenate(ys, axis=0) if n_sub > 1 else ys[0]

    inv_n = 1.0 / HEAD_DIM
    mu = _dot2l(y, seg) * inv_n
    d = y - mu
    var = _dot1(d * d, seg) * inv_n
    yn = d * lax.rsqrt(var + GN_EPS) * gnw_ref[...] + gnb_ref[...]
    o_ref[...] = ((yn + bonus) * gate).astype(BF16)


def _rwkv_mix(xs3, dbase, dup, ibase, iup, gup, keyk, keya, bonus, gnw, gnb, seg, tri, tc):
    bsz, seq, ncol = xs3.shape
    const = lambda b, i: (0, 0)
    vec = pl.BlockSpec((1, BRANCH_WIDTH), const)
    return pl.pallas_call(
        functools.partial(_rwkv_kernel, n_sub=tc // CHUNK),
        grid=(bsz, seq // tc),
        in_specs=[
            pl.BlockSpec((None, tc, ncol), lambda b, i: (b, i, 0)),
            vec,
            pl.BlockSpec(dup.shape, const),
            vec,
            pl.BlockSpec(iup.shape, const),
            pl.BlockSpec(gup.shape, const),
            vec, vec, vec, vec, vec,
            pl.BlockSpec(seg.shape, const),
            pl.BlockSpec(tri.shape, const),
        ],
        out_specs=pl.BlockSpec((None, tc, BRANCH_WIDTH), lambda b, i: (b, i, 0)),
        out_shape=jax.ShapeDtypeStruct((bsz, seq, BRANCH_WIDTH), BF16),
        scratch_shapes=[pltpu.VMEM((N_GROUPS, LANES, LANES), F32)],
        compiler_params=pltpu.CompilerParams(
            dimension_semantics=("parallel", "arbitrary"), vmem_limit_bytes=VMEM_LIMIT),
        name="rwkv_mix",
    )(xs3, dbase, dup, ibase, iup, gup, keyk, keya, bonus, gnw, gnb, seg, tri)


def _merge_kernel(x_ref, osb_ref, orw_ref, pg_ref, bg_ref, wb0_ref, wb1_ref, wout_ref,
                  g2_ref, x1_ref, h2_ref):
    d = x_ref.shape[1]
    up0 = _dot(osb_ref[...], wb0_ref[...])
    up1 = _dot(orw_ref[...], wb1_ref[...])
    gates = _sigmoid(pg_ref[...].astype(F32) + bg_ref[...])
    mixed = gates[:, :d] * up0 + gates[:, d:] * up1
    x1 = x_ref[...] + _dot(mixed.astype(BF16), wout_ref[...])
    x1_ref[...] = x1
    h2 = x1 * lax.rsqrt(jnp.mean(x1 * x1, axis=-1, keepdims=True) + NORM_EPS) * g2_ref[...]
    h2_ref[...] = h2.astype(BF16)


def _merge(x2, o_sb, o_rw, p_gate, bg, wb0, wb1, w_out, g2, tm):
    t, d = x2.shape
    const = lambda i: (0, 0)
    row = lambda i: (i, 0)
    return pl.pallas_call(
        _merge_kernel,
        grid=(t // tm,),
        in_specs=[
            pl.BlockSpec((tm, d), row),
            pl.BlockSpec((tm, BRANCH_WIDTH), row),
            pl.BlockSpec((tm, BRANCH_WIDTH), row),
            pl.BlockSpec((tm, 2 * d), row),
            pl.BlockSpec((1, 2 * d), const),
            pl.BlockSpec(wb0.shape, const),
            pl.BlockSpec(wb1.shape, const),
            pl.BlockSpec(w_out.shape, const),
            pl.BlockSpec((1, d), const),
        ],
        out_specs=[pl.BlockSpec((tm, d), row), pl.BlockSpec((tm, d), row)],
        out_shape=[jax.ShapeDtypeStruct((t, d), F32), jax.ShapeDtypeStruct((t, d), BF16)],
        compiler_params=pltpu.CompilerParams(
            dimension_semantics=("parallel",), vmem_limit_bytes=VMEM_LIMIT),
        name="merge_outproj",
    )(x2, o_sb, o_rw, p_gate, bg, wb0, wb1, w_out, g2)


def _ffn_kernel(h2_ref, x1_ref, wup_ref, cw_ref, cb_ref, wdown_ref, o_ref, carry_ref,
                *, tiles_per_seq, d_ff, nc):
    i = pl.program_id(0)
    first = (i % tiles_per_seq) == 0
    h2 = h2_ref[...]
    tm = h2.shape[0]
    row = lax.broadcasted_iota(jnp.int32, (tm, nc), 0)

    def conv(u, col0):
        cs = slice(col0, col0 + nc)
        old = jnp.where(first, 0.0, carry_ref[:, cs])
        p1 = jnp.where(row == 0, old[7:8, :], pltpu.roll(u, 1, 0))
        p2 = jnp.where(row == 0, old[6:7, :],
                       jnp.where(row == 1, old[7:8, :], pltpu.roll(u, 2, 0)))
        carry_ref[:, cs] = u[tm - 8:, :]
        cw = cw_ref[:, cs]
        return cw[0:1] * p2 + cw[1:2] * p1 + cw[2:3] * u + cb_ref[:, cs]

    def up(ch):
        return (_dot(h2, wup_ref[:, ch * nc:(ch + 1) * nc]),
                _dot(h2, wup_ref[:, d_ff + ch * nc:d_ff + (ch + 1) * nc]))

    n_ch = d_ff // nc
    acc = x1_ref[...]
    ahead = 3
    pending = [up(ch) for ch in range(min(ahead, n_ch))]
    for ch in range(n_ch):
        ug, uv = pending.pop(0)
        if ch + ahead < n_ch:
            pending.append(up(ch + ahead))
        cg = conv(ug, ch * nc)
        cv = conv(uv, d_ff + ch * nc)
        act = cg * _sigmoid(cg) * cv
        acc = acc + _dot(act.astype(BF16), wdown_ref[ch * nc:(ch + 1) * nc, :])
    o_ref[...] = acc


def _ffn(h2, x1, w_up, conv_w, conv_b, w_down, seq, tm, nc):
    t, d = x1.shape
    d_ff = w_down.shape[0]
    const = lambda i: (0, 0)
    row = lambda i: (i, 0)
    return pl.pallas_call(
        functools.partial(_ffn_kernel, tiles_per_seq=seq // tm, d_ff=d_ff, nc=nc),
        grid=(t // tm,),
        in_specs=[
            pl.BlockSpec((tm, d), row),
            pl.BlockSpec((tm, d), row),
            pl.BlockSpec(w_up.shape, const),
            pl.BlockSpec(conv_w.shape, const),
            pl.BlockSpec(conv_b.shape, const),
            pl.BlockSpec(w_down.shape, const),
        ],
        out_specs=pl.BlockSpec((tm, d), row),
        out_shape=jax.ShapeDtypeStruct((t, d), F32),
        scratch_shapes=[pltpu.VMEM((8, 2 * d_ff), F32)],
        compiler_params=pltpu.CompilerParams(
            dimension_semantics=("arbitrary",), vmem_limit_bytes=VMEM_LIMIT),
        name="ffn",
    )(h2, x1, w_up, conv_w, conv_b, w_down)


def _pad_cols(w, n):
    return jnp.pad(w, ((0, 0), (0, n - w.shape[1])))


def _pad_rows(w, n):
    return jnp.pad(w, ((0, n - w.shape[0]), (0, 0)))


def _layer(x, attn_norm_g, w_in, q_norm_g, k_norm_g, rwkv_shift_mu, decay_base, decay_up,
           iclr_base, iclr_up, out_gate_up, key_k, key_a, bonus_rk, group_norm_w,
           group_norm_b, branch_gate_b, w_branch, w_out, ffn_norm_g, w_ffn_up,
           ffn_conv_w, ffn_conv_b, w_ffn_down):
    bsz, seq, d = x.shape
    t = bsz * seq
    bw = BRANCH_WIDTH
    sb_cols = 3 * bw
    rw_cols = 3 * bw + DECAY_LORA + ICLR_LORA + GATE_LORA
    x2 = x.reshape(t, d)

    w_sb = w_in[:, :sb_cols].astype(BF16)
    w_rw_raw = w_in[:, sb_cols:sb_cols + rw_cols]
    mu_raw = rwkv_shift_mu.reshape(1, rw_cols)

    def regroup(m):
        o = 3 * bw
        return jnp.concatenate([
            m[:, :o],
            _pad_cols(m[:, o:o + DECAY_LORA], LORA_W_PAD),
            _pad_cols(m[:, o + DECAY_LORA:o + DECAY_LORA + ICLR_LORA], LORA_A_PAD),
            _pad_cols(m[:, o + DECAY_LORA + ICLR_LORA:], LORA_G_PAD),
        ], axis=1)

    w_rw = regroup(w_rw_raw).astype(BF16)
    mu = regroup(mu_raw)
    w_gate = w_in[:, sb_cols + rw_cols:].astype(BF16)

    p_sb, xs, p_gate = _in_projection(
        x2, attn_norm_g.reshape(1, d), w_sb, w_rw, w_gate, mu, seq, tm=256)

    blk = 256
    tri_sb = (jnp.arange(blk)[:, None] > jnp.arange(blk)[None, :]).astype(BF16)
    qg_pair = jnp.tile(q_norm_g.reshape(1, HEAD_DIM), (1, HEADS_PER_GROUP))
    kg_pair = jnp.tile(k_norm_g.reshape(1, HEAD_DIM), (1, HEADS_PER_GROUP))
    o_sb = _sb_attention(p_sb.reshape(bsz, seq, sb_cols), qg_pair, kg_pair, tri_sb, blk)

    head_id = jnp.arange(bw) // HEAD_DIM
    seg = (head_id[:, None] == head_id[None, :]).astype(BF16)
    tri_rw = (jnp.arange(CHUNK)[:, None] >= jnp.arange(CHUNK)[None, :]).astype(BF16)
    vec = lambda p: p.reshape(1, bw)
    o_rw = _rwkv_mix(
        xs.reshape(bsz, seq, RW_COLS_PAD), vec(decay_base), _pad_rows(decay_up, LORA_W_PAD),
        vec(iclr_base), _pad_rows(iclr_up, LORA_A_PAD), _pad_rows(out_gate_up, LORA_G_PAD),
        vec(key_k), vec(key_a), vec(bonus_rk), vec(group_norm_w), vec(group_norm_b),
        seg, tri_rw, tc=256)

    x1, h2 = _merge(
        x2, o_sb.reshape(t, bw), o_rw.reshape(t, bw), p_gate,
        branch_gate_b.reshape(1, 2 * d), w_branch[0].astype(BF16), w_branch[1].astype(BF16),
        w_out.astype(BF16), ffn_norm_g.reshape(1, d), tm=512)

    out = _ffn(h2, x1, w_ffn_up.astype(BF16), ffn_conv_w, ffn_conv_b.reshape(1, -1),
               w_ffn_down.astype(BF16), seq, tm=256, nc=256)
    return out.reshape(bsz, seq, d)


def kernel(x, attn_norm_g, w_in, q_norm_g, k_norm_g, rwkv_shift_mu, decay_base, decay_up,
           iclr_base, iclr_up, out_gate_up, key_k, key_a, bonus_rk, group_norm_w,
           group_norm_b, branch_gate_b, w_branch, w_out, ffn_norm_g, w_ffn_up, ffn_conv_w,
           ffn_conv_b, w_ffn_down):
    for layer in range(attn_norm_g.shape[0]):
        x = _layer(
            x, attn_norm_g[layer], w_in[layer], q_norm_g[layer], k_norm_g[layer],
            rwkv_shift_mu[layer], decay_base[layer], decay_up[layer], iclr_base[layer],
            iclr_up[layer], out_gate_up[layer], key_k[layer], key_a[layer], bonus_rk[layer],
            group_norm_w[layer], group_norm_b[layer], branch_gate_b[layer], w_branch[layer],
            w_out[layer], ffn_norm_g[layer], w_ffn_up[layer], ffn_conv_w[layer],
            ffn_conv_b[layer], w_ffn_down[layer])
    return x
```

```python
import functools

import jax
import jax.numpy as jnp
from jax import lax
from jax.experimental import pallas as pl
from jax.experimental.pallas import tpu as pltpu

F32 = jnp.float32
BF16 = jnp.bfloat16

HEAD_DIM = 64
N_HEADS = 8
BRANCH_WIDTH = N_HEADS * HEAD_DIM
LANES = 128
HEADS_PER_GROUP = LANES // HEAD_DIM
N_GROUPS = BRANCH_WIDTH // LANES
DECAY_LORA = 64
ICLR_LORA = 64
GATE_LORA = 160
LORA_W_PAD = 128
LORA_A_PAD = 128
LORA_G_PAD = 256
RW_COLS_PAD = 3 * BRANCH_WIDTH + LORA_W_PAD + LORA_A_PAD + LORA_G_PAD
CONV_WIDTH = 3
NORM_EPS = 1e-6
GN_EPS = 64e-5
LOG2E = 1.4426950408889634
MASKED_LOG2 = -1e30
CHUNK = 64
VMEM_LIMIT = 56 * 1024 * 1024


def _dot(a, b):
    return jnp.dot(a, b, preferred_element_type=F32)


def _dot_t(a, b):
    return lax.dot_general(a, b, (((1,), (1,)), ((), ())), preferred_element_type=F32)


def _split(a):
    hi = a.astype(BF16)
    lo = (a - hi.astype(F32)).astype(BF16)
    return hi, lo


def _dot3(a, b):
    ah, al = _split(a)
    bh, bl = _split(b)
    return _dot(ah, bh) + _dot(ah, bl) + _dot(al, bh)


def _dot1(a, b):
    return _dot(a.astype(BF16), b.astype(BF16))


def _dot2l(a, b_bf16):
    ah, al = _split(a)
    return _dot(ah, b_bf16) + _dot(al, b_bf16)


def _dot2r(a_bf16, b):
    bh, bl = _split(b)
    return _dot(a_bf16, bh) + _dot(a_bf16, bl)


def _sigmoid(x):
    return 1.0 / (1.0 + jnp.exp(-x))


def _softplus(x):
    return jnp.maximum(x, 0.0) + jnp.log(1.0 + jnp.exp(-jnp.abs(x)))


def _inproj_kernel(x_ref, g_ref, wsb_ref, wrw_ref, wgate_ref, mu_ref,
                   psb_ref, xs_ref, pgate_ref, carry_ref, *, tiles_per_seq):
    i = pl.program_id(0)
    x = x_ref[...]
    h = x * lax.rsqrt(jnp.mean(x * x, axis=-1, keepdims=True) + NORM_EPS) * g_ref[...]
    hb = h.astype(BF16)
    psb_ref[...] = _dot(hb, wsb_ref[...])
    pgate_ref[...] = _dot(hb, wgate_ref[...]).astype(BF16)
    p = _dot(hb, wrw_ref[...])
    first = (i % tiles_per_seq) == 0
    prev_last = jnp.where(first, 0.0, carry_ref[7:8, :])
    rolled = pltpu.roll(p, 1, 0)
    row = lax.broadcasted_iota(jnp.int32, p.shape, 0)
    prev = jnp.where(row == 0, prev_last, rolled)
    xs_ref[...] = p + (prev - p) * mu_ref[...]
    carry_ref[...] = p[p.shape[0] - 8:, :]


def _in_projection(x2, g, w_sb, w_rw, w_gate, mu, seq, tm):
    t, d = x2.shape
    n_sb, n_rw, n_gate = w_sb.shape[1], w_rw.shape[1], w_gate.shape[1]
    const = lambda i: (0, 0)
    row = lambda i: (i, 0)
    return pl.pallas_call(
        functools.partial(_inproj_kernel, tiles_per_seq=seq // tm),
        grid=(t // tm,),
        in_specs=[
            pl.BlockSpec((tm, d), row),
            pl.BlockSpec((1, d), const),
            pl.BlockSpec((d, n_sb), const),
            pl.BlockSpec((d, n_rw), const),
            pl.BlockSpec((d, n_gate), const),
            pl.BlockSpec((1, n_rw), const),
        ],
        out_specs=[
            pl.BlockSpec((tm, n_sb), row),
            pl.BlockSpec((tm, n_rw), row),
            pl.BlockSpec((tm, n_gate), row),
        ],
        out_shape=[
            jax.ShapeDtypeStruct((t, n_sb), F32),
            jax.ShapeDtypeStruct((t, n_rw), F32),
            jax.ShapeDtypeStruct((t, n_gate), BF16),
        ],
        scratch_shapes=[pltpu.VMEM((8, n_rw), F32)],
        compiler_params=pltpu.CompilerParams(
            dimension_semantics=("arbitrary",), vmem_limit_bytes=VMEM_LIMIT),
        name="in_projection",
    )(x2, g, w_sb, w_rw, w_gate, mu)


def _head_rmsnorm(x, g_pair, lane_lo):
    sq = x * x
    s_lo = jnp.sum(jnp.where(lane_lo, sq, 0.0), axis=-1, keepdims=True)
    s_all = jnp.sum(sq, axis=-1, keepdims=True)
    ms = jnp.where(lane_lo, s_lo, s_all - s_lo) * (1.0 / HEAD_DIM)
    return x * lax.rsqrt(ms + NORM_EPS) * g_pair


def _sb_attn_kernel(q_ref, k_ref, v_ref, qg_ref, kg_ref, tri_ref, o_ref,
                    kn_ref, vm_ref, z_ref, x_ref, keep_ref, acc_ref, carry_ref, *, blk, gps):
    qi = pl.program_id(2)
    heads = range(HEADS_PER_GROUP * gps)
    group_lanes = [slice(g * LANES, (g + 1) * LANES) for g in range(gps)]

    @pl.when(qi == 0)
    def _():
        lane_lo_k = lax.broadcasted_iota(jnp.int32, (k_ref.shape[0], LANES), 1) < HEAD_DIM
        for g, gl in enumerate(group_lanes):
            kn_ref[:, gl] = _head_rmsnorm(k_ref[:, gl], kg_ref[...], lane_lo_k).astype(BF16)
            v = v_ref[:, gl]
            vm_ref[2 * g] = jnp.where(lane_lo_k, v, 0.0).astype(BF16)
            vm_ref[2 * g + 1] = jnp.where(lane_lo_k, 0.0, v).astype(BF16)

    lane_lo = lax.broadcasted_iota(jnp.int32, (blk, LANES), 1) < HEAD_DIM
    q_heads = []
    for gl in group_lanes:
        qn = _head_rmsnorm(q_ref[:, gl], qg_ref[...], lane_lo) * (HEAD_DIM ** -0.5 * LOG2E)
        q_heads += [jnp.where(lane_lo, qn, 0.0).astype(BF16),
                    jnp.where(lane_lo, 0.0, qn).astype(BF16)]
    tri = tri_ref[...]
    t_idx = lax.broadcasted_iota(jnp.int32, (blk, blk), 0)
    s_idx = lax.broadcasted_iota(jnp.int32, (blk, blk), 1)
    strict = s_idx < t_idx
    lane_lo_half = lax.broadcasted_iota(jnp.int32, (blk // 2, LANES), 1) < HEAD_DIM

    def key_start(i):
        return pl.multiple_of(jnp.maximum(qi - i, 0) * blk, blk)

    def scores(start):
        kbs = [kn_ref[pl.ds(start, blk), gl] for gl in group_lanes]
        return [_dot_t(q_heads[hh], kbs[hh // HEADS_PER_GROUP]) for hh in heads]

    def front(zs, slot_w, diag):
        half = blk // 2
        for rows in (slice(0, half), slice(half, blk)):
            lss, lks = [], []
            for z in zs:
                z = z[rows]
                sp = (jnp.log(1.0 + jnp.exp2(-jnp.abs(z.astype(BF16)))) * LOG2E).astype(F32)
                ls = jnp.minimum(z, 0.0) - sp
                lk = ls - z
                lss.append(ls)
                lks.append(jnp.where(strict[rows], lk, 0.0) if diag else lk)
            afters = [_dot1(lk, tri) for lk in lks]
            sums = [jnp.sum(lk, axis=-1, keepdims=True) for lk in lks]
            for g, gl in enumerate(group_lanes):
                sum_pair = jnp.where(lane_lo_half, sums[2 * g], sums[2 * g + 1])
                if diag:
                    keep_ref[slot_w, rows, gl] = jnp.ones((half, LANES), F32)
                    carry_ref[rows, gl] = sum_pair
                else:
                    carry = carry_ref[rows, gl]
                    keep_ref[slot_w, rows, gl] = jnp.exp2(carry)
                    carry_ref[rows, gl] = carry + sum_pair
            for hh in heads:
                x = lss[hh] + afters[hh]
                x_ref[slot_w, hh, rows, :] = (
                    jnp.where(strict[rows], x, MASKED_LOG2) if diag else x)

    def back(start, slot_r):
        ws = [jnp.exp2(x_ref[slot_r, hh]).astype(BF16) for hh in heads]
        for g, gl in enumerate(group_lanes):
            pv = sum(_dot(ws[hh], vm_ref[hh, pl.ds(start, blk), :])
                     for hh in (2 * g, 2 * g + 1))
            acc_ref[:, gl] = acc_ref[:, gl] + keep_ref[slot_r, :, gl] * pv

    def step(i, slot):
        zs = [z_ref[slot, hh] for hh in heads]
        nxt = scores(key_start(i + 1))
        for hh in heads:
            z_ref[1 - slot, hh] = nxt[hh]
        back(key_start(i - 1), 1 - slot)
        front(zs, slot, False)

    acc_ref[...] = jnp.zeros_like(acc_ref)
    front(scores(key_start(0)), 0, True)
    nxt = scores(key_start(1))
    for hh in heads:
        z_ref[1, hh] = nxt[hh]

    def body(t, _):
        step(2 * t + 1, 1)

        @pl.when(2 * t + 2 <= qi)
        def _():
            step(2 * t + 2, 0)

        return 0

    lax.fori_loop(0, (qi + 1) // 2, body, 0)
    back(key_start(qi), qi % 2)
    o_ref[...] = acc_ref[...].astype(BF16)


def _sb_attention(p_sb3, qg_pair, kg_pair, tri, blk, gps):
    bsz, seq, _ = p_sb3.shape
    nq = seq // blk
    width = gps * LANES
    kblk0 = BRANCH_WIDTH // width
    n_heads_step = HEADS_PER_GROUP * gps
    return pl.pallas_call(
        functools.partial(_sb_attn_kernel, blk=blk, gps=gps),
        grid=(bsz, N_GROUPS // gps, nq),
        in_specs=[
            pl.BlockSpec((None, blk, width), lambda b, g, i: (b, i, g)),
            pl.BlockSpec((None, seq, width), lambda b, g, i: (b, 0, kblk0 + g)),
            pl.BlockSpec((None, seq, width), lambda b, g, i: (b, 0, 2 * kblk0 + g)),
            pl.BlockSpec((1, LANES), lambda b, g, i: (0, 0)),
            pl.BlockSpec((1, LANES), lambda b, g, i: (0, 0)),
            pl.BlockSpec((blk, blk), lambda b, g, i: (0, 0)),
        ],
        out_specs=pl.BlockSpec((None, blk, width), lambda b, g, i: (b, i, g)),
        out_shape=jax.ShapeDtypeStruct((bsz, seq, BRANCH_WIDTH), BF16),
        scratch_shapes=[
            pltpu.VMEM((seq, width), BF16),
            pltpu.VMEM((n_heads_step, seq, LANES), BF16),
            pltpu.VMEM((2, n_heads_step, blk, blk), F32),
            pltpu.VMEM((2, n_heads_step, blk, blk), F32),
            pltpu.VMEM((2, blk, width), F32),
            pltpu.VMEM((blk, width), F32),
            pltpu.VMEM((blk, width), F32),
        ],
        compiler_params=pltpu.CompilerParams(
            dimension_semantics=("parallel", "parallel", "arbitrary"),
            vmem_limit_bytes=VMEM_LIMIT),
        name="sb_attention",
    )(p_sb3, p_sb3, p_sb3, qg_pair, kg_pair, tri)


def _rwkv_kernel(xs_ref, dbase_ref, dup_ref, ibase_ref, iup_ref, gup_ref, keyk_ref,
                 keya_ref, bonus_ref, gnw_ref, gnb_ref, seg_ref, tri_ref, o_ref,
                 h_ref, *, n_sub):
    c = CHUNK
    bw = BRANCH_WIDTH

    @pl.when(pl.program_id(1) == 0)
    def _():
        h_ref[...] = jnp.zeros_like(h_ref)

    xs = xs_ref[...]
    r = xs[:, 0:bw]
    k = xs[:, bw:2 * bw]
    v = xs[:, 2 * bw:3 * bw]
    o0 = 3 * bw
    w_lo = xs[:, o0:o0 + LORA_W_PAD]
    a_lo = xs[:, o0 + LORA_W_PAD:o0 + LORA_W_PAD + LORA_A_PAD]
    g_lo = xs[:, o0 + LORA_W_PAD + LORA_A_PAD:]
    seg = seg_ref[...]

    def seg_sum(x, passes):
        parts = []
        for g in range(N_GROUPS):
            xg = x[:, g * LANES:(g + 1) * LANES]
            parts.append(_dot2l(xg, seg) if passes == 2 else _dot1(xg, seg))
        return jnp.concatenate(parts, axis=-1)


    dw = dbase_ref[...] + _dot3(jnp.tanh(w_lo), dup_ref[...])
    w_log = -_softplus(-dw) - 0.5
    lw = -jnp.exp(w_log)
    iclr = _sigmoid(ibase_ref[...] + _dot1(a_lo, iup_ref[...]))
    gate = _dot1(_sigmoid(g_lo), gup_ref[...])
    kk0 = k * keyk_ref[...]
    kk = kk0 / jnp.maximum(jnp.sqrt(seg_sum(kk0 * kk0, 1)), 1e-12)
    kmod = k * (1.0 + (iclr - 1.0) * keya_ref[...])
    a = -kk
    b = kk * iclr
    bonus = seg_sum(r * kmod * bonus_ref[...], 1) * v

    tri = tri_ref[...]
    row = lax.broadcasted_iota(jnp.int32, (LANES, LANES), 0)
    col = lax.broadcasted_iota(jnp.int32, (LANES, LANES), 1)
    same_head = (row // c) == (col // c)
    strict_lo = jnp.logical_and(same_head, (col % c) < (row % c))
    incl_lo = jnp.logical_and(same_head, (col % c) <= (row % c))
    eye = row == col
    lane = lax.broadcasted_iota(jnp.int32, (c, LANES), 1)
    head_masks = [(lane // HEAD_DIM) == hh for hh in range(HEADS_PER_GROUP)]

    def stack(x):
        return jnp.concatenate([jnp.where(m, x, 0.0) for m in head_masks], axis=0)

    chains = [(s, g) for s in range(n_sub) for g in range(N_GROUPS)]
    ops = {}
    for s in range(n_sub):
        sl = slice(s * c, (s + 1) * c)
        lw_s = lw[sl]
        cum = _dot2r(tri, lw_s)
        tot = cum[c - 1:c, :]
        e_cum = jnp.exp(cum)
        e_prev = jnp.exp(cum - lw_s)
        e_inv = jnp.exp(-cum)
        e_rem = jnp.exp(tot - cum)
        p_tot = jnp.exp(tot)
        full = dict(a=a[sl] * e_prev, r=r[sl] * e_cum, b=b[sl] * e_inv, k=kmod[sl] * e_inv,
                    bh=b[sl] * e_rem, kh=kmod[sl] * e_rem, v=v[sl])
        for g in range(N_GROUPS):
            ls = slice(g * LANES, (g + 1) * LANES)
            st = {name: stack(val[:, ls]) for name, val in full.items()}
            ops[s, g] = dict(
                a=st["a"].astype(BF16), b=st["b"].astype(BF16), k=st["k"].astype(BF16),
                v=st["v"].astype(BF16), r=st["r"], bh_t=st["bh"].T.astype(BF16),
                kh_t=st["kh"].T.astype(BF16), p_tot=p_tot[:, ls])

    l_pow, t_inv, l_ak, m_rb, m_rk = {}, {}, {}, {}, {}
    for ch in chains:
        o = ops[ch]
        sc = _dot_t(jnp.concatenate([o["a"], o["r"].astype(BF16)], axis=0),
                    jnp.concatenate([o["b"], o["k"]], axis=0))
        l_ab = jnp.where(strict_lo, sc[:2 * c, :2 * c], 0.0)
        l_ak[ch] = jnp.where(strict_lo, sc[:2 * c, 2 * c:], 0.0).astype(BF16)
        m_rb[ch] = jnp.where(incl_lo, sc[2 * c:, :2 * c], 0.0).astype(BF16)
        m_rk[ch] = jnp.where(incl_lo, sc[2 * c:, 2 * c:], 0.0).astype(BF16)
        l_pow[ch] = l_ab
        t_inv[ch] = jnp.where(eye, 1.0, l_ab)
    for _ in range(5):
        for ch in chains:
            lp = l_pow[ch].astype(BF16)
            l_pow[ch] = _dot(lp, lp)
        for ch in chains:
            t_inv[ch] = t_inv[ch] + _dot1(t_inv[ch], l_pow[ch])
    rh2, yl2, m_mat, n_mat, p_col = {}, {}, {}, {}, {}
    x_loc = {ch: _dot(l_ak[ch], ops[ch]["v"]) for ch in chains}
    zw = {ch: _dot(t_inv[ch].astype(BF16),
                   jnp.concatenate([ops[ch]["a"], x_loc[ch].astype(BF16)], axis=1))
          for ch in chains}
    for ch in chains:
        o = ops[ch]
        ap2 = zw[ch][:, :LANES].astype(BF16)
        wt2 = zw[ch][:, LANES:].astype(BF16)
        rh2[ch] = o["r"] + _dot(m_rb[ch], ap2)
        yl2[ch] = _dot(m_rb[ch], wt2) + _dot(m_rk[ch], o["v"])
        m_mat[ch] = _dot(o["bh_t"], ap2).astype(BF16)
        p_col[ch] = jnp.sum(jnp.where(eye, o["p_tot"], 0.0), axis=1, keepdims=True)
        n_mat[ch] = _dot(o["bh_t"], wt2) + _dot(o["kh_t"], o["v"])

    ys = []
    h = [h_ref[g] for g in range(N_GROUPS)]
    for s in range(n_sub):
        y_groups = []
        for g in range(N_GROUPS):
            y2 = _dot1(rh2[s, g], h[g]) + yl2[s, g]
            h[g] = p_col[s, g] * h[g] + _dot1(m_mat[s, g], h[g]) + n_mat[s, g]
            y_groups.append(y2[:c] + y2[c:])
        ys.append(jnp.concatenate(y_groups, axis=-1))
    for g in range(N_GROUPS):
        h_ref[g] = h[g]
    y = jnp.concatenate(ys, axis=0) if n_sub > 1 else ys[0]

    inv_n = 1.0 / HEAD_DIM
    mu = seg_sum(y, 2) * inv_n
    d = y - mu
    var = seg_sum(d * d, 1) * inv_n
    yn = d * lax.rsqrt(var + GN_EPS) * gnw_ref[...] + gnb_ref[...]
    o_ref[...] = ((yn + bonus) * gate).astype(BF16)


def _rwkv_mix(xs3, dbase, dup, ibase, iup, gup, keyk, keya, bonus, gnw, gnb, seg, tri, tc):
    bsz, seq, ncol = xs3.shape
    const = lambda b, i: (0, 0)
    vec = pl.BlockSpec((1, BRANCH_WIDTH), const)
    return pl.pallas_call(
        functools.partial(_rwkv_kernel, n_sub=tc // CHUNK),
        grid=(bsz, seq // tc),
        in_specs=[
            pl.BlockSpec((None, tc, ncol), lambda b, i: (b, i, 0)),
            vec,
            pl.BlockSpec(dup.shape, const),
            vec,
            pl.BlockSpec(iup.shape, const),
            pl.BlockSpec(gup.shape, const),
            vec, vec, vec, vec, vec,
            pl.BlockSpec(seg.shape, const),
            pl.BlockSpec(tri.shape, const),
        ],
        out_specs=pl.BlockSpec((None, tc, BRANCH_WIDTH), lambda b, i: (b, i, 0)),
        out_shape=jax.ShapeDtypeStruct((bsz, seq, BRANCH_WIDTH), BF16),
        scratch_shapes=[pltpu.VMEM((N_GROUPS, LANES, LANES), F32)],
        compiler_params=pltpu.CompilerParams(
            dimension_semantics=("parallel", "arbitrary"), vmem_limit_bytes=VMEM_LIMIT),
        name="rwkv_mix",
    )(xs3, dbase, dup, ibase, iup, gup, keyk, keya, bonus, gnw, gnb, seg, tri)


def _merge_kernel(x_ref, osb_ref, orw_ref, pg_ref, bg_ref, wb0_ref, wb1_ref, wout_ref,
                  g2_ref, x1_ref, h2_ref):
    d = x_ref.shape[1]
    up0 = _dot(osb_ref[...], wb0_ref[...])
    up1 = _dot(orw_ref[...], wb1_ref[...])
    gates = _sigmoid(pg_ref[...].astype(F32) + bg_ref[...])
    mixed = gates[:, :d] * up0 + gates[:, d:] * up1
    x1 = x_ref[...] + _dot(mixed.astype(BF16), wout_ref[...])
    x1_ref[...] = x1
    h2 = x1 * lax.rsqrt(jnp.mean(x1 * x1, axis=-1, keepdims=True) + NORM_EPS) * g2_ref[...]
    h2_ref[...] = h2.astype(BF16)


def _merge(x2, o_sb, o_rw, p_gate, bg, wb0, wb1, w_out, g2, tm):
    t, d = x2.shape
    const = lambda i: (0, 0)
    row = lambda i: (i, 0)
    return pl.pallas_call(
        _merge_kernel,
        grid=(t // tm,),
        in_specs=[
            pl.BlockSpec((tm, d), row),
            pl.BlockSpec((tm, BRANCH_WIDTH), row),
            pl.BlockSpec((tm, BRANCH_WIDTH), row),
            pl.BlockSpec((tm, 2 * d), row),
            pl.BlockSpec((1, 2 * d), const),
            pl.BlockSpec(wb0.shape, const),
            pl.BlockSpec(wb1.shape, const),
            pl.BlockSpec(w_out.shape, const),
            pl.BlockSpec((1, d), const),
        ],
        out_specs=[pl.BlockSpec((tm, d), row), pl.BlockSpec((tm, d), row)],
        out_shape=[jax.ShapeDtypeStruct((t, d), F32), jax.ShapeDtypeStruct((t, d), BF16)],
        compiler_params=pltpu.CompilerParams(
            dimension_semantics=("parallel",), vmem_limit_bytes=VMEM_LIMIT),
        name="merge_outproj",
    )(x2, o_sb, o_rw, p_gate, bg, wb0, wb1, w_out, g2)


def _ffn_kernel(h2_ref, x1_ref, wup_ref, cw_ref, cb_ref, wdown_ref, o_ref, carry_ref,
                *, tiles_per_seq, d_ff, nc):
    i = pl.program_id(0)
    first = (i % tiles_per_seq) == 0
    h2 = h2_ref[...]
    tm = h2.shape[0]
    row = lax.broadcasted_iota(jnp.int32, (tm, nc), 0)

    def conv(u, col0):
        cs = slice(col0, col0 + nc)
        old = jnp.where(first, 0.0, carry_ref[:, cs])
        p1 = jnp.where(row == 0, old[7:8, :], pltpu.roll(u, 1, 0))
        p2 = jnp.where(row == 0, old[6:7, :],
                       jnp.where(row == 1, old[7:8, :], pltpu.roll(u, 2, 0)))
        carry_ref[:, cs] = u[tm - 8:, :]
        cw = cw_ref[:, cs]
        return cw[0:1] * p2 + cw[1:2] * p1 + cw[2:3] * u + cb_ref[:, cs]

    def up(ch):
        return (_dot(h2, wup_ref[:, ch * nc:(ch + 1) * nc]),
                _dot(h2, wup_ref[:, d_ff + ch * nc:d_ff + (ch + 1) * nc]))

    n_ch = d_ff // nc
    acc = x1_ref[...]
    ahead = 3
    pending = [up(ch) for ch in range(min(ahead, n_ch))]
    for ch in range(n_ch):
        ug, uv = pending.pop(0)
        if ch + ahead < n_ch:
            pending.append(up(ch + ahead))
        cg = conv(ug, ch * nc)
        cv = conv(uv, d_ff + ch * nc)
        act = cg * _sigmoid(cg) * cv
        acc = acc + _dot(act.astype(BF16), wdown_ref[ch * nc:(ch + 1) * nc, :])
    o_ref[...] = acc


def _ffn(h2, x1, w_up, conv_w, conv_b, w_down, seq, tm, nc):
    t, d = x1.shape
    d_ff = w_down.shape[0]
    const = lambda i: (0, 0)
    row = lambda i: (i, 0)
    return pl.pallas_call(
        functools.partial(_ffn_kernel, tiles_per_seq=seq // tm, d_ff=d_ff, nc=nc),
        grid=(t // tm,),
        in_specs=[
            pl.BlockSpec((tm, d), row),
            pl.BlockSpec((tm, d), row),
            pl.BlockSpec(w_up.shape, const),
            pl.BlockSpec(conv_w.shape, const),
            pl.BlockSpec(conv_b.shape, const),
            pl.BlockSpec(w_down.shape, const),
        ],
        out_specs=pl.BlockSpec((tm, d), row),
        out_shape=jax.ShapeDtypeStruct((t, d), F32),
        scratch_shapes=[pltpu.VMEM((8, 2 * d_ff), F32)],
        compiler_params=pltpu.CompilerParams(
            dimension_semantics=("arbitrary",), vmem_limit_bytes=VMEM_LIMIT),
        name="ffn",
    )(h2, x1, w_up, conv_w, conv_b, w_down)


def _pad_cols(w, n):
    return jnp.pad(w, ((0, 0), (0, n - w.shape[1])))


def _pad_rows(w, n):
    return jnp.pad(w, ((0, n - w.shape[0]), (0, 0)))


def _layer(x, attn_norm_g, w_in, q_norm_g, k_norm_g, rwkv_shift_mu, decay_base, decay_up,
           iclr_base, iclr_up, out_gate_up, key_k, key_a, bonus_rk, group_norm_w,
           group_norm_b, branch_gate_b, w_branch, w_out, ffn_norm_g, w_ffn_up,
           ffn_conv_w, ffn_conv_b, w_ffn_down):
    bsz, seq, d = x.shape
    t = bsz * seq
    bw = BRANCH_WIDTH
    sb_cols = 3 * bw
    rw_cols = 3 * bw + DECAY_LORA + ICLR_LORA + GATE_LORA
    x2 = x.reshape(t, d)

    w_sb = w_in[:, :sb_cols].astype(BF16)
    w_rw_raw = w_in[:, sb_cols:sb_cols + rw_cols]
    mu_raw = rwkv_shift_mu.reshape(1, rw_cols)

    def regroup(m):
        o = 3 * bw
        return jnp.concatenate([
            m[:, :o],
            _pad_cols(m[:, o:o + DECAY_LORA], LORA_W_PAD),
            _pad_cols(m[:, o + DECAY_LORA:o + DECAY_LORA + ICLR_LORA], LORA_A_PAD),
            _pad_cols(m[:, o + DECAY_LORA + ICLR_LORA:], LORA_G_PAD),
        ], axis=1)

    w_rw = regroup(w_rw_raw).astype(BF16)
    mu = regroup(mu_raw)
    w_gate = w_in[:, sb_cols + rw_cols:].astype(BF16)

    p_sb, xs, p_gate = _in_projection(
        x2, attn_norm_g.reshape(1, d), w_sb, w_rw, w_gate, mu, seq, tm=256)

    blk = 256
    tri_sb = (jnp.arange(blk)[:, None] > jnp.arange(blk)[None, :]).astype(BF16)
    qg_pair = jnp.tile(q_norm_g.reshape(1, HEAD_DIM), (1, HEADS_PER_GROUP))
    kg_pair = jnp.tile(k_norm_g.reshape(1, HEAD_DIM), (1, HEADS_PER_GROUP))
    o_sb = _sb_attention(p_sb.reshape(bsz, seq, sb_cols), qg_pair, kg_pair, tri_sb, blk, gps=4)

    head_id = jnp.arange(LANES) // HEAD_DIM
    seg = (head_id[:, None] == head_id[None, :]).astype(BF16)
    tri_rw = (jnp.arange(CHUNK)[:, None] >= jnp.arange(CHUNK)[None, :]).astype(BF16)
    vec = lambda p: p.reshape(1, bw)
    o_rw = _rwkv_mix(
        xs.reshape(bsz, seq, RW_COLS_PAD), vec(decay_base), _pad_rows(decay_up, LORA_W_PAD),
        vec(iclr_base), _pad_rows(iclr_up, LORA_A_PAD), _pad_rows(out_gate_up, LORA_G_PAD),
        vec(key_k), vec(key_a), vec(bonus_rk), vec(group_norm_w), vec(group_norm_b),
        seg, tri_rw, tc=256)

    x1, h2 = _merge(
        x2, o_sb.reshape(t, bw), o_rw.reshape(t, bw), p_gate,
        branch_gate_b.reshape(1, 2 * d), w_branch[0].astype(BF16), w_branch[1].astype(BF16),
        w_out.astype(BF16), ffn_norm_g.reshape(1, d), tm=512)

    out = _ffn(h2, x1, w_ffn_up.astype(BF16), ffn_conv_w, ffn_conv_b.reshape(1, -1),
               w_ffn_down.astype(BF16), seq, tm=256, nc=256)
    return out.reshape(bsz, seq, d)


def kernel(x, attn_norm_g, w_in, q_norm_g, k_norm_g, rwkv_shift_mu, decay_base, decay_up,
           iclr_base, iclr_up, out_gate_up, key_k, key_a, bonus_rk, group_norm_w,
           group_norm_b, branch_gate_b, w_branch, w_out, ffn_norm_g, w_ffn_up, ffn_conv_w,
           ffn_conv_b, w_ffn_down):
    for layer in range(attn_norm_g.shape[0]):
        x = _layer(
            x, attn_norm_g[layer], w_in[layer], q_norm_g[layer], k_norm_g[layer],
            rwkv_shift_mu[layer], decay_base[layer], decay_up[layer], iclr_base[layer],
            iclr_up[layer], out_gate_up[layer], key_k[layer], key_a[layer], bonus_rk[layer],
            group_norm_w[layer], group_norm_b[layer], branch_gate_b[layer], w_branch[layer],
            w_out[layer], ffn_norm_g[layer], w_ffn_up[layer], ffn_conv_w[layer],
            ffn_conv_b[layer], w_ffn_down[layer])
    return x
```

```python
import functools

import jax
import jax.numpy as jnp
from jax import lax
from jax.experimental import pallas as pl
from jax.experimental.pallas import tpu as pltpu

F32 = jnp.float32
BF16 = jnp.bfloat16

HEAD_DIM = 64
N_HEADS = 8
BRANCH_WIDTH = N_HEADS * HEAD_DIM
LANES = 128
HEADS_PER_GROUP = LANES // HEAD_DIM
N_GROUPS = BRANCH_WIDTH // LANES
DECAY_LORA = 64
ICLR_LORA = 64
GATE_LORA = 160
LORA_W_PAD = 128
LORA_A_PAD = 128
LORA_G_PAD = 256
RW_COLS_PAD = 3 * BRANCH_WIDTH + LORA_W_PAD + LORA_A_PAD + LORA_G_PAD
CONV_WIDTH = 3
NORM_EPS = 1e-6
GN_EPS = 64e-5
LOG2E = 1.4426950408889634
MASKED_LOG2 = -1e30
CHUNK = 64
VMEM_LIMIT = 56 * 1024 * 1024


def _dot(a, b):
    return jnp.dot(a, b, preferred_element_type=F32)


def _dot_t(a, b):
    return lax.dot_general(a, b, (((1,), (1,)), ((), ())), preferred_element_type=F32)


def _split(a):
    hi = a.astype(BF16)
    lo = (a - hi.astype(F32)).astype(BF16)
    return hi, lo


def _dot3(a, b):
    ah, al = _split(a)
    bh, bl = _split(b)
    return _dot(ah, bh) + _dot(ah, bl) + _dot(al, bh)


def _dot1(a, b):
    return _dot(a.astype(BF16), b.astype(BF16))


def _dot2l(a, b_bf16):
    ah, al = _split(a)
    return _dot(ah, b_bf16) + _dot(al, b_bf16)


def _dot2r(a_bf16, b):
    bh, bl = _split(b)
    return _dot(a_bf16, bh) + _dot(a_bf16, bl)


def _sigmoid(x):
    return 1.0 / (1.0 + jnp.exp(-x))


def _softplus(x):
    return jnp.maximum(x, 0.0) + jnp.log(1.0 + jnp.exp(-jnp.abs(x)))


def _inproj_kernel(x_ref, g_ref, wsb_ref, wrw_ref, wgate_ref, mu_ref,
                   psb_ref, xs_ref, pgate_ref, carry_ref, *, tiles_per_seq, sub):
    i = pl.program_id(0)
    first_tile = (i % tiles_per_seq) == 0
    row = lax.broadcasted_iota(jnp.int32, (sub, xs_ref.shape[1]), 0)
    for s in range(x_ref.shape[0] // sub):
        r = slice(s * sub, (s + 1) * sub)
        x = x_ref[r, :]
        h = x * lax.rsqrt(jnp.mean(x * x, axis=-1, keepdims=True) + NORM_EPS) * g_ref[...]
        hb = h.astype(BF16)
        psb_ref[r, :] = _dot(hb, wsb_ref[...])
        pgate_ref[r, :] = _dot(hb, wgate_ref[...]).astype(BF16)
        p = _dot(hb, wrw_ref[...])
        prev_last = carry_ref[7:8, :]
        if s == 0:
            prev_last = jnp.where(first_tile, 0.0, prev_last)
        prev = jnp.where(row == 0, prev_last, pltpu.roll(p, 1, 0))
        xs_ref[r, :] = p + (prev - p) * mu_ref[...]
        carry_ref[...] = p[sub - 8:, :]


def _in_projection(x2, g, w_sb, w_rw, w_gate, mu, seq, tm, sub):
    t, d = x2.shape
    n_sb, n_rw, n_gate = w_sb.shape[1], w_rw.shape[1], w_gate.shape[1]
    const = lambda i: (0, 0)
    row = lambda i: (i, 0)
    return pl.pallas_call(
        functools.partial(_inproj_kernel, tiles_per_seq=seq // tm, sub=sub),
        grid=(t // tm,),
        in_specs=[
            pl.BlockSpec((tm, d), row),
            pl.BlockSpec((1, d), const),
            pl.BlockSpec((d, n_sb), const),
            pl.BlockSpec((d, n_rw), const),
            pl.BlockSpec((d, n_gate), const),
            pl.BlockSpec((1, n_rw), const),
        ],
        out_specs=[
            pl.BlockSpec((tm, n_sb), row),
            pl.BlockSpec((tm, n_rw), row),
            pl.BlockSpec((tm, n_gate), row),
        ],
        out_shape=[
            jax.ShapeDtypeStruct((t, n_sb), F32),
            jax.ShapeDtypeStruct((t, n_rw), F32),
            jax.ShapeDtypeStruct((t, n_gate), BF16),
        ],
        scratch_shapes=[pltpu.VMEM((8, n_rw), F32)],
        compiler_params=pltpu.CompilerParams(
            dimension_semantics=("arbitrary",), vmem_limit_bytes=VMEM_LIMIT),
        name="in_projection",
    )(x2, g, w_sb, w_rw, w_gate, mu)


def _head_rmsnorm(x, g_pair, lane_lo):
    sq = x * x
    s_lo = jnp.sum(jnp.where(lane_lo, sq, 0.0), axis=-1, keepdims=True)
    s_all = jnp.sum(sq, axis=-1, keepdims=True)
    ms = jnp.where(lane_lo, s_lo, s_all - s_lo) * (1.0 / HEAD_DIM)
    return x * lax.rsqrt(ms + NORM_EPS) * g_pair


def _sb_attn_kernel(q_ref, k_ref, v_ref, qg_ref, kg_ref, tri_ref, o_ref,
                    kn_ref, vm_ref, z_ref, x_ref, keep_ref, acc_ref, carry_ref, *, blk, gps):
    qi = pl.program_id(2)
    heads = range(HEADS_PER_GROUP * gps)
    group_lanes = [slice(g * LANES, (g + 1) * LANES) for g in range(gps)]

    @pl.when(qi == 0)
    def _():
        lane_lo_k = lax.broadcasted_iota(jnp.int32, (k_ref.shape[0], LANES), 1) < HEAD_DIM
        for g, gl in enumerate(group_lanes):
            kn_ref[:, gl] = _head_rmsnorm(k_ref[:, gl], kg_ref[...], lane_lo_k).astype(BF16)
            v = v_ref[:, gl]
            vm_ref[2 * g] = jnp.where(lane_lo_k, v, 0.0).astype(BF16)
            vm_ref[2 * g + 1] = jnp.where(lane_lo_k, 0.0, v).astype(BF16)

    lane_lo = lax.broadcasted_iota(jnp.int32, (blk, LANES), 1) < HEAD_DIM
    q_heads = []
    for gl in group_lanes:
        qn = _head_rmsnorm(q_ref[:, gl], qg_ref[...], lane_lo) * (HEAD_DIM ** -0.5 * LOG2E)
        q_heads += [jnp.where(lane_lo, qn, 0.0).astype(BF16),
                    jnp.where(lane_lo, 0.0, qn).astype(BF16)]
    tri = tri_ref[...]
    t_idx = lax.broadcasted_iota(jnp.int32, (blk, blk), 0)
    s_idx = lax.broadcasted_iota(jnp.int32, (blk, blk), 1)
    strict = s_idx < t_idx
    lane_lo_half = lax.broadcasted_iota(jnp.int32, (blk // 2, LANES), 1) < HEAD_DIM

    def key_start(i):
        return pl.multiple_of(jnp.maximum(qi - i, 0) * blk, blk)

    def scores(start):
        kbs = [kn_ref[pl.ds(start, blk), gl] for gl in group_lanes]
        return [_dot_t(q_heads[hh], kbs[hh // HEADS_PER_GROUP]) for hh in heads]

    def front(zs, slot_w, diag):
        half = blk // 2
        for rows in (slice(0, half), slice(half, blk)):
            lss, lks = [], []
            for z in zs:
                z = z[rows]
                sp = (jnp.log(1.0 + jnp.exp2(-jnp.abs(z.astype(BF16)))) * LOG2E).astype(F32)
                ls = jnp.minimum(z, 0.0) - sp
                lk = ls - z
                lss.append(ls)
                lks.append(jnp.where(strict[rows], lk, 0.0) if diag else lk)
            afters = [_dot1(lk, tri) for lk in lks]
            sums = [jnp.sum(lk, axis=-1, keepdims=True) for lk in lks]
            for g, gl in enumerate(group_lanes):
                sum_pair = jnp.where(lane_lo_half, sums[2 * g], sums[2 * g + 1])
                if diag:
                    keep_ref[slot_w, rows, gl] = jnp.ones((half, LANES), F32)
                    carry_ref[rows, gl] = sum_pair
                else:
                    carry = carry_ref[rows, gl]
                    keep_ref[slot_w, rows, gl] = jnp.exp2(carry)
                    carry_ref[rows, gl] = carry + sum_pair
            for hh in heads:
                x = lss[hh] + afters[hh]
                x_ref[slot_w, hh, rows, :] = (
                    jnp.where(strict[rows], x, MASKED_LOG2) if diag else x)

    def back(start, slot_r):
        ws = [jnp.exp2(x_ref[slot_r, hh]).astype(BF16) for hh in heads]
        for g, gl in enumerate(group_lanes):
            pv = sum(_dot(ws[hh], vm_ref[hh, pl.ds(start, blk), :])
                     for hh in (2 * g, 2 * g + 1))
            acc_ref[:, gl] = acc_ref[:, gl] + keep_ref[slot_r, :, gl] * pv

    def step(i, slot):
        zs = [z_ref[slot, hh] for hh in heads]
        nxt = scores(key_start(i + 1))
        for hh in heads:
            z_ref[1 - slot, hh] = nxt[hh]
        back(key_start(i - 1), 1 - slot)
        front(zs, slot, False)

    acc_ref[...] = jnp.zeros_like(acc_ref)
    front(scores(key_start(0)), 0, True)
    nxt = scores(key_start(1))
    for hh in heads:
        z_ref[1, hh] = nxt[hh]

    def body(t, _):
        step(2 * t + 1, 1)

        @pl.when(2 * t + 2 <= qi)
        def _():
            step(2 * t + 2, 0)

        return 0

    lax.fori_loop(0, (qi + 1) // 2, body, 0)
    back(key_start(qi), qi % 2)
    o_ref[...] = acc_ref[...].astype(BF16)


def _sb_attention(p_sb3, qg_pair, kg_pair, tri, blk, gps):
    bsz, seq, _ = p_sb3.shape
    nq = seq // blk
    width = gps * LANES
    kblk0 = BRANCH_WIDTH // width
    n_heads_step = HEADS_PER_GROUP * gps
    return pl.pallas_call(
        functools.partial(_sb_attn_kernel, blk=blk, gps=gps),
        grid=(bsz, N_GROUPS // gps, nq),
        in_specs=[
            pl.BlockSpec((None, blk, width), lambda b, g, i: (b, i, g)),
            pl.BlockSpec((None, seq, width), lambda b, g, i: (b, 0, kblk0 + g)),
            pl.BlockSpec((None, seq, width), lambda b, g, i: (b, 0, 2 * kblk0 + g)),
            pl.BlockSpec((1, LANES), lambda b, g, i: (0, 0)),
            pl.BlockSpec((1, LANES), lambda b, g, i: (0, 0)),
            pl.BlockSpec((blk, blk), lambda b, g, i: (0, 0)),
        ],
        out_specs=pl.BlockSpec((None, blk, width), lambda b, g, i: (b, i, g)),
        out_shape=jax.ShapeDtypeStruct((bsz, seq, BRANCH_WIDTH), BF16),
        scratch_shapes=[
            pltpu.VMEM((seq, width), BF16),
            pltpu.VMEM((n_heads_step, seq, LANES), BF16),
            pltpu.VMEM((2, n_heads_step, blk, blk), F32),
            pltpu.VMEM((2, n_heads_step, blk, blk), F32),
            pltpu.VMEM((2, blk, width), F32),
            pltpu.VMEM((blk, width), F32),
            pltpu.VMEM((blk, width), F32),
        ],
        compiler_params=pltpu.CompilerParams(
            dimension_semantics=("parallel", "parallel", "arbitrary"),
            vmem_limit_bytes=VMEM_LIMIT),
        name="sb_attention",
    )(p_sb3, p_sb3, p_sb3, qg_pair, kg_pair, tri)


def _rwkv_kernel(xs_ref, dbase_ref, dup_ref, ibase_ref, iup_ref, gup_ref, keyk_ref,
                 keya_ref, bonus_ref, gnw_ref, gnb_ref, seg_ref, tri_ref, o_ref,
                 h_ref, *, n_sub):
    c = CHUNK
    bw = BRANCH_WIDTH

    @pl.when(pl.program_id(1) == 0)
    def _():
        h_ref[...] = jnp.zeros_like(h_ref)

    xs = xs_ref[...]
    r = xs[:, 0:bw]
    k = xs[:, bw:2 * bw]
    v = xs[:, 2 * bw:3 * bw]
    o0 = 3 * bw
    w_lo = xs[:, o0:o0 + LORA_W_PAD]
    a_lo = xs[:, o0 + LORA_W_PAD:o0 + LORA_W_PAD + LORA_A_PAD]
    g_lo = xs[:, o0 + LORA_W_PAD + LORA_A_PAD:]
    seg = seg_ref[...]

    def seg_sum(x, passes):
        parts = []
        for g in range(N_GROUPS):
            xg = x[:, g * LANES:(g + 1) * LANES]
            parts.append(_dot2l(xg, seg) if passes == 2 else _dot1(xg, seg))
        return jnp.concatenate(parts, axis=-1)


    dw = dbase_ref[...] + _dot3(jnp.tanh(w_lo), dup_ref[...])
    w_log = -_softplus(-dw) - 0.5
    lw = -jnp.exp(w_log)
    iclr = _sigmoid(ibase_ref[...] + _dot1(a_lo, iup_ref[...]))
    gate = _dot1(_sigmoid(g_lo), gup_ref[...])
    kk0 = k * keyk_ref[...]
    kk = kk0 / jnp.maximum(jnp.sqrt(seg_sum(kk0 * kk0, 1)), 1e-12)
    kmod = k * (1.0 + (iclr - 1.0) * keya_ref[...])
    a = -kk
    b = kk * iclr
    bonus = seg_sum(r * kmod * bonus_ref[...], 1) * v

    tri = tri_ref[...]
    row = lax.broadcasted_iota(jnp.int32, (LANES, LANES), 0)
    col = lax.broadcasted_iota(jnp.int32, (LANES, LANES), 1)
    same_head = (row // c) == (col // c)
    strict_lo = jnp.logical_and(same_head, (col % c) < (row % c))
    incl_lo = jnp.logical_and(same_head, (col % c) <= (row % c))
    eye = row == col
    lane = lax.broadcasted_iota(jnp.int32, (c, LANES), 1)
    head_masks = [(lane // HEAD_DIM) == hh for hh in range(HEADS_PER_GROUP)]

    def stack(x):
        return jnp.concatenate([jnp.where(m, x, 0.0) for m in head_masks], axis=0)

    chains = [(s, g) for s in range(n_sub) for g in range(N_GROUPS)]
    ops = {}
    for s in range(n_sub):
        sl = slice(s * c, (s + 1) * c)
        lw_s = lw[sl]
        cum = _dot2r(tri, lw_s)
        tot = cum[c - 1:c, :]
        e_cum = jnp.exp(cum)
        e_prev = jnp.exp(cum - lw_s)
        e_inv = jnp.exp(-cum)
        e_rem = jnp.exp(tot - cum)
        p_tot = jnp.exp(tot)
        full = dict(a=a[sl] * e_prev, r=r[sl] * e_cum, b=b[sl] * e_inv, k=kmod[sl] * e_inv,
                    bh=b[sl] * e_rem, kh=kmod[sl] * e_rem, v=v[sl])
        for g in range(N_GROUPS):
            ls = slice(g * LANES, (g + 1) * LANES)
            st = {name: stack(val[:, ls]) for name, val in full.items()}
            ops[s, g] = dict(
                a=st["a"].astype(BF16), b=st["b"].astype(BF16), k=st["k"].astype(BF16),
                v=st["v"].astype(BF16), r=st["r"], bh_t=st["bh"].T.astype(BF16),
                kh_t=st["kh"].T.astype(BF16), p_tot=p_tot[:, ls])

    l_pow, t_inv, l_ak, m_rb, m_rk = {}, {}, {}, {}, {}
    for ch in chains:
        o = ops[ch]
        sc = _dot_t(jnp.concatenate([o["a"], o["r"].astype(BF16)], axis=0),
                    jnp.concatenate([o["b"], o["k"]], axis=0))
        l_ab = jnp.where(strict_lo, sc[:2 * c, :2 * c], 0.0)
        l_ak[ch] = jnp.where(strict_lo, sc[:2 * c, 2 * c:], 0.0).astype(BF16)
        m_rb[ch] = jnp.where(incl_lo, sc[2 * c:, :2 * c], 0.0).astype(BF16)
        m_rk[ch] = jnp.where(incl_lo, sc[2 * c:, 2 * c:], 0.0).astype(BF16)
        l_pow[ch] = l_ab
        t_inv[ch] = jnp.where(eye, 1.0, l_ab)
    for _ in range(5):
        for ch in chains:
            lp = l_pow[ch].astype(BF16)
            l_pow[ch] = _dot(lp, lp)
        for ch in chains:
            t_inv[ch] = t_inv[ch] + _dot1(t_inv[ch], l_pow[ch])
    rh2, yl2, m_mat, n_mat, p_col = {}, {}, {}, {}, {}
    x_loc = {ch: _dot(l_ak[ch], ops[ch]["v"]) for ch in chains}
    zw = {ch: _dot(t_inv[ch].astype(BF16),
                   jnp.concatenate([ops[ch]["a"], x_loc[ch].astype(BF16)], axis=1))
          for ch in chains}
    for ch in chains:
        o = ops[ch]
        ap2 = zw[ch][:, :LANES].astype(BF16)
        wt2 = zw[ch][:, LANES:].astype(BF16)
        rh2[ch] = o["r"] + _dot(m_rb[ch], ap2)
        yl2[ch] = _dot(m_rb[ch], wt2) + _dot(m_rk[ch], o["v"])
        m_mat[ch] = _dot(o["bh_t"], ap2).astype(BF16)
        p_col[ch] = jnp.sum(jnp.where(eye, o["p_tot"], 0.0), axis=1, keepdims=True)
        n_mat[ch] = _dot(o["bh_t"], wt2) + _dot(o["kh_t"], o["v"])

    ys = []
    h = [h_ref[g] for g in range(N_GROUPS)]
    for s in range(n_sub):
        y_groups = []
        for g in range(N_GROUPS):
            y2 = _dot1(rh2[s, g], h[g]) + yl2[s, g]
            h[g] = p_col[s, g] * h[g] + _dot1(m_mat[s, g], h[g]) + n_mat[s, g]
            y_groups.append(y2[:c] + y2[c:])
        ys.append(jnp.concatenate(y_groups, axis=-1))
    for g in range(N_GROUPS):
        h_ref[g] = h[g]
    y = jnp.concatenate(ys, axis=0) if n_sub > 1 else ys[0]

    inv_n = 1.0 / HEAD_DIM
    mu = seg_sum(y, 2) * inv_n
    d = y - mu
    var = seg_sum(d * d, 1) * inv_n
    yn = d * lax.rsqrt(var + GN_EPS) * gnw_ref[...] + gnb_ref[...]
    o_ref[...] = ((yn + bonus) * gate).astype(BF16)


def _rwkv_mix(xs3, dbase, dup, ibase, iup, gup, keyk, keya, bonus, gnw, gnb, seg, tri, tc):
    bsz, seq, ncol = xs3.shape
    const = lambda b, i: (0, 0)
    vec = pl.BlockSpec((1, BRANCH_WIDTH), const)
    return pl.pallas_call(
        functools.partial(_rwkv_kernel, n_sub=tc // CHUNK),
        grid=(bsz, seq // tc),
        in_specs=[
            pl.BlockSpec((None, tc, ncol), lambda b, i: (b, i, 0)),
            vec,
            pl.BlockSpec(dup.shape, const),
            vec,
            pl.BlockSpec(iup.shape, const),
            pl.BlockSpec(gup.shape, const),
            vec, vec, vec, vec, vec,
            pl.BlockSpec(seg.shape, const),
            pl.BlockSpec(tri.shape, const),
        ],
        out_specs=pl.BlockSpec((None, tc, BRANCH_WIDTH), lambda b, i: (b, i, 0)),
        out_shape=jax.ShapeDtypeStruct((bsz, seq, BRANCH_WIDTH), BF16),
        scratch_shapes=[pltpu.VMEM((N_GROUPS, LANES, LANES), F32)],
        compiler_params=pltpu.CompilerParams(
            dimension_semantics=("parallel", "arbitrary"), vmem_limit_bytes=VMEM_LIMIT),
        name="rwkv_mix",
    )(xs3, dbase, dup, ibase, iup, gup, keyk, keya, bonus, gnw, gnb, seg, tri)


def _merge_kernel(x_ref, osb_ref, orw_ref, pg_ref, bg_ref, wb0_ref, wb1_ref, wout_ref,
                  g2_ref, x1_ref, h2_ref):
    d = x_ref.shape[1]
    up0 = _dot(osb_ref[...], wb0_ref[...])
    up1 = _dot(orw_ref[...], wb1_ref[...])
    gates = _sigmoid(pg_ref[...].astype(F32) + bg_ref[...])
    mixed = gates[:, :d] * up0 + gates[:, d:] * up1
    x1 = x_ref[...] + _dot(mixed.astype(BF16), wout_ref[...])
    x1_ref[...] = x1
    h2 = x1 * lax.rsqrt(jnp.mean(x1 * x1, axis=-1, keepdims=True) + NORM_EPS) * g2_ref[...]
    h2_ref[...] = h2.astype(BF16)


def _merge(x2, o_sb, o_rw, p_gate, bg, wb0, wb1, w_out, g2, tm):
    t, d = x2.shape
    const = lambda i: (0, 0)
    row = lambda i: (i, 0)
    return pl.pallas_call(
        _merge_kernel,
        grid=(t // tm,),
        in_specs=[
            pl.BlockSpec((tm, d), row),
            pl.BlockSpec((tm, BRANCH_WIDTH), row),
            pl.BlockSpec((tm, BRANCH_WIDTH), row),
            pl.BlockSpec((tm, 2 * d), row),
            pl.BlockSpec((1, 2 * d), const),
            pl.BlockSpec(wb0.shape, const),
            pl.BlockSpec(wb1.shape, const),
            pl.BlockSpec(w_out.shape, const),
            pl.BlockSpec((1, d), const),
        ],
        out_specs=[pl.BlockSpec((tm, d), row), pl.BlockSpec((tm, d), row)],
        out_shape=[jax.ShapeDtypeStruct((t, d), F32), jax.ShapeDtypeStruct((t, d), BF16)],
        compiler_params=pltpu.CompilerParams(
            dimension_semantics=("parallel",), vmem_limit_bytes=VMEM_LIMIT),
        name="merge_outproj",
    )(x2, o_sb, o_rw, p_gate, bg, wb0, wb1, w_out, g2)


def _ffn_kernel(h2_ref, x1_ref, wup_ref, cw_ref, cb_ref, wdown_ref, o_ref, carry_ref,
                *, tiles_per_seq, d_ff, nc, sub):
    i = pl.program_id(0)
    first_tile = (i % tiles_per_seq) == 0
    tm = h2_ref.shape[0]
    n_sub = tm // sub
    n_ch = d_ff // nc
    row = lax.broadcasted_iota(jnp.int32, (sub, nc), 0)
    rows = [slice(s * sub, (s + 1) * sub) for s in range(n_sub)]
    h2s = [h2_ref[r, :] for r in rows]

    def conv(u, col0, seq_start):
        cs = slice(col0, col0 + nc)
        old = carry_ref[:, cs]
        if seq_start is not None:
            old = jnp.where(seq_start, 0.0, old)
        p1 = jnp.where(row == 0, old[7:8, :], pltpu.roll(u, 1, 0))
        p2 = jnp.where(row == 0, old[6:7, :],
                       jnp.where(row == 1, old[7:8, :], pltpu.roll(u, 2, 0)))
        carry_ref[:, cs] = u[sub - 8:, :]
        cw = cw_ref[:, cs]
        return cw[0:1] * p2 + cw[1:2] * p1 + cw[2:3] * u + cb_ref[:, cs]

    def up(item):
        s, ch = item
        return (_dot(h2s[s], wup_ref[:, ch * nc:(ch + 1) * nc]),
                _dot(h2s[s], wup_ref[:, d_ff + ch * nc:d_ff + (ch + 1) * nc]))

    items = [(s, ch) for s in range(n_sub) for ch in range(n_ch)]
    accs = [x1_ref[r, :] for r in rows]
    ahead = 3
    pending = [up(it) for it in items[:ahead]]
    for k, (s, ch) in enumerate(items):
        ug, uv = pending.pop(0)
        if k + ahead < len(items):
            pending.append(up(items[k + ahead]))
        seq_start = first_tile if s == 0 else None
        cg = conv(ug, ch * nc, seq_start)
        cv = conv(uv, d_ff + ch * nc, seq_start)
        act = cg * _sigmoid(cg) * cv
        accs[s] = accs[s] + _dot(act.astype(BF16), wdown_ref[ch * nc:(ch + 1) * nc, :])
    for s, r in enumerate(rows):
        o_ref[r, :] = accs[s]


def _ffn(h2, x1, w_up, conv_w, conv_b, w_down, seq, tm, nc, sub):
    t, d = x1.shape
    d_ff = w_down.shape[0]
    const = lambda i: (0, 0)
    row = lambda i: (i, 0)
    return pl.pallas_call(
        functools.partial(_ffn_kernel, tiles_per_seq=seq // tm, d_ff=d_ff, nc=nc, sub=sub),
        grid=(t // tm,),
        in_specs=[
            pl.BlockSpec((tm, d), row),
            pl.BlockSpec((tm, d), row),
            pl.BlockSpec(w_up.shape, const),
            pl.BlockSpec(conv_w.shape, const),
            pl.BlockSpec(conv_b.shape, const),
            pl.BlockSpec(w_down.shape, const),
        ],
        out_specs=pl.BlockSpec((tm, d), row),
        out_shape=jax.ShapeDtypeStruct((t, d), F32),
        scratch_shapes=[pltpu.VMEM((8, 2 * d_ff), F32)],
        compiler_params=pltpu.CompilerParams(
            dimension_semantics=("arbitrary",), vmem_limit_bytes=VMEM_LIMIT),
        name="ffn",
    )(h2, x1, w_up, conv_w, conv_b, w_down)


def _pad_cols(w, n):
    return jnp.pad(w, ((0, 0), (0, n - w.shape[1])))


def _pad_rows(w, n):
    return jnp.pad(w, ((0, n - w.shape[0]), (0, 0)))


def _layer(x, attn_norm_g, w_in, q_norm_g, k_norm_g, rwkv_shift_mu, decay_base, decay_up,
           iclr_base, iclr_up, out_gate_up, key_k, key_a, bonus_rk, group_norm_w,
           group_norm_b, branch_gate_b, w_branch, w_out, ffn_norm_g, w_ffn_up,
           ffn_conv_w, ffn_conv_b, w_ffn_down):
    bsz, seq, d = x.shape
    t = bsz * seq
    bw = BRANCH_WIDTH
    sb_cols = 3 * bw
    rw_cols = 3 * bw + DECAY_LORA + ICLR_LORA + GATE_LORA
    x2 = x.reshape(t, d)

    w_sb = w_in[:, :sb_cols].astype(BF16)
    w_rw_raw = w_in[:, sb_cols:sb_cols + rw_cols]
    mu_raw = rwkv_shift_mu.reshape(1, rw_cols)

    def regroup(m):
        o = 3 * bw
        return jnp.concatenate([
            m[:, :o],
            _pad_cols(m[:, o:o + DECAY_LORA], LORA_W_PAD),
            _pad_cols(m[:, o + DECAY_LORA:o + DECAY_LORA + ICLR_LORA], LORA_A_PAD),
            _pad_cols(m[:, o + DECAY_LORA + ICLR_LORA:], LORA_G_PAD),
        ], axis=1)

    w_rw = regroup(w_rw_raw).astype(BF16)
    mu = regroup(mu_raw)
    w_gate = w_in[:, sb_cols + rw_cols:].astype(BF16)

    p_sb, xs, p_gate = _in_projection(
        x2, attn_norm_g.reshape(1, d), w_sb, w_rw, w_gate, mu, seq, tm=512, sub=256)

    blk = 256
    tri_sb = (jnp.arange(blk)[:, None] > jnp.arange(blk)[None, :]).astype(BF16)
    qg_pair = jnp.tile(q_norm_g.reshape(1, HEAD_DIM), (1, HEADS_PER_GROUP))
    kg_pair = jnp.tile(k_norm_g.reshape(1, HEAD_DIM), (1, HEADS_PER_GROUP))
    o_sb = _sb_attention(p_sb.reshape(bsz, seq, sb_cols), qg_pair, kg_pair, tri_sb, blk, gps=4)

    head_id = jnp.arange(LANES) // HEAD_DIM
    seg = (head_id[:, None] == head_id[None, :]).astype(BF16)
    tri_rw = (jnp.arange(CHUNK)[:, None] >= jnp.arange(CHUNK)[None, :]).astype(BF16)
    vec = lambda p: p.reshape(1, bw)
    o_rw = _rwkv_mix(
        xs.reshape(bsz, seq, RW_COLS_PAD), vec(decay_base), _pad_rows(decay_up, LORA_W_PAD),
        vec(iclr_base), _pad_rows(iclr_up, LORA_A_PAD), _pad_rows(out_gate_up, LORA_G_PAD),
        vec(key_k), vec(key_a), vec(bonus_rk), vec(group_norm_w), vec(group_norm_b),
        seg, tri_rw, tc=512)

    x1, h2 = _merge(
        x2, o_sb.reshape(t, bw), o_rw.reshape(t, bw), p_gate,
        branch_gate_b.reshape(1, 2 * d), w_branch[0].astype(BF16), w_branch[1].astype(BF16),
        w_out.astype(BF16), ffn_norm_g.reshape(1, d), tm=512)

    out = _ffn(h2, x1, w_ffn_up.astype(BF16), ffn_conv_w, ffn_conv_b.reshape(1, -1),
               w_ffn_down.astype(BF16), seq, tm=512, nc=256, sub=256)
    return out.reshape(bsz, seq, d)


def kernel(x, attn_norm_g, w_in, q_norm_g, k_norm_g, rwkv_shift_mu, decay_base, decay_up,
           iclr_base, iclr_up, out_gate_up, key_k, key_a, bonus_rk, group_norm_w,
           group_norm_b, branch_gate_b, w_branch, w_out, ffn_norm_g, w_ffn_up, ffn_conv_w,
           ffn_conv_b, w_ffn_down):
    for layer in range(attn_norm_g.shape[0]):
        x = _layer(
            x, attn_norm_g[layer], w_in[layer], q_norm_g[layer], k_norm_g[layer],
            rwkv_shift_mu[layer], decay_base[layer], decay_up[layer], iclr_base[layer],
            iclr_up[layer], out_gate_up[layer], key_k[layer], key_a[layer], bonus_rk[layer],
            group_norm_w[layer], group_norm_b[layer], branch_gate_b[layer], w_branch[layer],
            w_out[layer], ffn_norm_g[layer], w_ffn_up[layer], ffn_conv_w[layer],
            ffn_conv_b[layer], w_ffn_down[layer])
    return x
```

```python
import functools

import jax
import jax.numpy as jnp
from jax import lax
from jax.experimental import pallas as pl
from jax.experimental.pallas import tpu as pltpu

F32 = jnp.float32
BF16 = jnp.bfloat16

HEAD_DIM = 64
N_HEADS = 8
BRANCH_WIDTH = N_HEADS * HEAD_DIM
LANES = 128
HEADS_PER_GROUP = LANES // HEAD_DIM
N_GROUPS = BRANCH_WIDTH // LANES
DECAY_LORA = 64
ICLR_LORA = 64
GATE_LORA = 160
LORA_W_PAD = 128
LORA_A_PAD = 128
LORA_G_PAD = 256
RW_COLS_PAD = 3 * BRANCH_WIDTH + LORA_W_PAD + LORA_A_PAD + LORA_G_PAD
CONV_WIDTH = 3
NORM_EPS = 1e-6
GN_EPS = 64e-5
LOG2E = 1.4426950408889634
MASKED_LOG2 = -1e30
CHUNK = 64
VMEM_LIMIT = 56 * 1024 * 1024


def _dot(a, b):
    return jnp.dot(a, b, preferred_element_type=F32)


def _dot_t(a, b):
    return lax.dot_general(a, b, (((1,), (1,)), ((), ())), preferred_element_type=F32)


def _split(a):
    hi = a.astype(BF16)
    lo = (a - hi.astype(F32)).astype(BF16)
    return hi, lo


def _dot3(a, b):
    ah, al = _split(a)
    bh, bl = _split(b)
    return _dot(ah, bh) + _dot(ah, bl) + _dot(al, bh)


def _dot1(a, b):
    return _dot(a.astype(BF16), b.astype(BF16))


def _dot2l(a, b_bf16):
    ah, al = _split(a)
    return _dot(ah, b_bf16) + _dot(al, b_bf16)


def _dot2r(a_bf16, b):
    bh, bl = _split(b)
    return _dot(a_bf16, bh) + _dot(a_bf16, bl)


def _sigmoid(x):
    return 1.0 / (1.0 + jnp.exp(-x))


def _softplus(x):
    return jnp.maximum(x, 0.0) + jnp.log(1.0 + jnp.exp(-jnp.abs(x)))


def _inproj_kernel(x_ref, g_ref, wsb_ref, wrw_ref, wgate_ref, mu_ref, qg_ref, kg_ref,
                   psb_ref, xs_ref, pgate_ref, carry_ref, *, tiles_per_seq, sub):
    i = pl.program_id(0)
    first_tile = (i % tiles_per_seq) == 0
    row = lax.broadcasted_iota(jnp.int32, (sub, xs_ref.shape[1]), 0)
    lane_lo = lax.broadcasted_iota(jnp.int32, (sub, LANES), 1) < HEAD_DIM
    for s in range(x_ref.shape[0] // sub):
        r = slice(s * sub, (s + 1) * sub)
        x = x_ref[r, :]
        h = x * lax.rsqrt(jnp.mean(x * x, axis=-1, keepdims=True) + NORM_EPS) * g_ref[...]
        hb = h.astype(BF16)
        p_sb = _dot(hb, wsb_ref[...])
        for j in range(BRANCH_WIDTH // LANES):
            for base, gain, scale in ((0, qg_ref, HEAD_DIM ** -0.5 * LOG2E), (BRANCH_WIDTH, kg_ref, 1.0)):
                cols = slice(base + j * LANES, base + (j + 1) * LANES)
                normed = _head_rmsnorm(p_sb[:, cols], gain[...], lane_lo)
                psb_ref[r, cols] = (normed * scale).astype(BF16)
        psb_ref[r, 2 * BRANCH_WIDTH:] = p_sb[:, 2 * BRANCH_WIDTH:].astype(BF16)
        pgate_ref[r, :] = _dot(hb, wgate_ref[...]).astype(BF16)
        p = _dot(hb, wrw_ref[...])
        prev_last = carry_ref[7:8, :]
        if s == 0:
            prev_last = jnp.where(first_tile, 0.0, prev_last)
        prev = jnp.where(row == 0, prev_last, pltpu.roll(p, 1, 0))
        xs_ref[r, :] = p + (prev - p) * mu_ref[...]
        carry_ref[...] = p[sub - 8:, :]


def _in_projection(x2, g, w_sb, w_rw, w_gate, mu, qg_pair, kg_pair, seq, tm, sub):
    t, d = x2.shape
    n_sb, n_rw, n_gate = w_sb.shape[1], w_rw.shape[1], w_gate.shape[1]
    const = lambda i: (0, 0)
    row = lambda i: (i, 0)
    return pl.pallas_call(
        functools.partial(_inproj_kernel, tiles_per_seq=seq // tm, sub=sub),
        grid=(t // tm,),
        in_specs=[
            pl.BlockSpec((tm, d), row),
            pl.BlockSpec((1, d), const),
            pl.BlockSpec((d, n_sb), const),
            pl.BlockSpec((d, n_rw), const),
            pl.BlockSpec((d, n_gate), const),
            pl.BlockSpec((1, n_rw), const),
            pl.BlockSpec((1, LANES), const),
            pl.BlockSpec((1, LANES), const),
        ],
        out_specs=[
            pl.BlockSpec((tm, n_sb), row),
            pl.BlockSpec((tm, n_rw), row),
            pl.BlockSpec((tm, n_gate), row),
        ],
        out_shape=[
            jax.ShapeDtypeStruct((t, n_sb), BF16),
            jax.ShapeDtypeStruct((t, n_rw), F32),
            jax.ShapeDtypeStruct((t, n_gate), BF16),
        ],
        scratch_shapes=[pltpu.VMEM((8, n_rw), F32)],
        compiler_params=pltpu.CompilerParams(
            dimension_semantics=("arbitrary",), vmem_limit_bytes=VMEM_LIMIT),
        name="in_projection",
    )(x2, g, w_sb, w_rw, w_gate, mu, qg_pair, kg_pair)


def _head_rmsnorm(x, g_pair, lane_lo):
    sq = x * x
    s_lo = jnp.sum(jnp.where(lane_lo, sq, 0.0), axis=-1, keepdims=True)
    s_all = jnp.sum(sq, axis=-1, keepdims=True)
    ms = jnp.where(lane_lo, s_lo, s_all - s_lo) * (1.0 / HEAD_DIM)
    return x * lax.rsqrt(ms + NORM_EPS) * g_pair


def _sb_attn_kernel(q_ref, k_ref, v_ref, tri_ref, o_ref,
                    vm_ref, z_ref, x_ref, keep_ref, acc_ref, carry_ref, *, blk, gps):
    qi = pl.program_id(2)
    heads = range(HEADS_PER_GROUP * gps)
    group_lanes = [slice(g * LANES, (g + 1) * LANES) for g in range(gps)]

    @pl.when(qi == 0)
    def _():
        lane_lo_k = lax.broadcasted_iota(jnp.int32, (k_ref.shape[0], LANES), 1) < HEAD_DIM
        for g, gl in enumerate(group_lanes):
            v = v_ref[:, gl]
            zero = jnp.zeros_like(v)
            vm_ref[2 * g] = jnp.where(lane_lo_k, v, zero)
            vm_ref[2 * g + 1] = jnp.where(lane_lo_k, zero, v)

    lane_lo = lax.broadcasted_iota(jnp.int32, (blk, LANES), 1) < HEAD_DIM
    q_heads = []
    for gl in group_lanes:
        qn = q_ref[:, gl]
        zero = jnp.zeros_like(qn)
        q_heads += [jnp.where(lane_lo, qn, zero), jnp.where(lane_lo, zero, qn)]
    tri = tri_ref[...]
    t_idx = lax.broadcasted_iota(jnp.int32, (blk, blk), 0)
    s_idx = lax.broadcasted_iota(jnp.int32, (blk, blk), 1)
    strict = s_idx < t_idx
    row_parts = 2
    lane_lo_half = lax.broadcasted_iota(jnp.int32, (blk // row_parts, LANES), 1) < HEAD_DIM

    def key_start(i):
        return pl.multiple_of(jnp.maximum(qi - i, 0) * blk, blk)

    def scores(start):
        kbs = [k_ref[pl.ds(start, blk), gl] for gl in group_lanes]
        return [_dot_t(q_heads[hh], kbs[hh // HEADS_PER_GROUP]) for hh in heads]

    def front(zs, slot_w, diag, groups):
        half = blk // row_parts
        hs = [hh for g in groups for hh in (2 * g, 2 * g + 1)]
        for rows in [slice(p * half, (p + 1) * half) for p in range(row_parts)]:
            lss, lks = {}, {}
            for hh in hs:
                z = zs[hh][rows]
                sp = (jnp.log(1.0 + jnp.exp2(-jnp.abs(z.astype(BF16)))) * LOG2E).astype(F32)
                ls = jnp.minimum(z, 0.0) - sp
                lk = ls - z
                lss[hh] = ls
                lks[hh] = jnp.where(strict[rows], lk, 0.0) if diag else lk
            afters = {hh: _dot1(lks[hh], tri) for hh in hs}
            sums = {hh: jnp.sum(lks[hh], axis=-1, keepdims=True) for hh in hs}
            for g in groups:
                gl = group_lanes[g]
                sum_pair = jnp.where(lane_lo_half, sums[2 * g], sums[2 * g + 1])
                if diag:
                    keep_ref[slot_w, rows, gl] = jnp.ones((half, LANES), F32)
                    carry_ref[rows, gl] = sum_pair
                else:
                    carry = carry_ref[rows, gl]
                    keep_ref[slot_w, rows, gl] = jnp.exp2(carry)
                    carry_ref[rows, gl] = carry + sum_pair
            for hh in hs:
                x = lss[hh] + afters[hh]
                x_ref[slot_w, hh, rows, :] = (
                    jnp.where(strict[rows], x, MASKED_LOG2) if diag else x)

    def back(start, slot_r, groups):
        for g in groups:
            gl = group_lanes[g]
            ws = {hh: jnp.exp2(x_ref[slot_r, hh]).astype(BF16) for hh in (2 * g, 2 * g + 1)}
            pv = sum(_dot(ws[hh], vm_ref[hh, pl.ds(start, blk), :])
                     for hh in (2 * g, 2 * g + 1))
            acc_ref[:, gl] = acc_ref[:, gl] + keep_ref[slot_r, :, gl] * pv

    def step(i, slot):
        zs = [z_ref[slot, hh] for hh in heads]
        nxt = scores(key_start(i + 1))
        for hh in heads:
            z_ref[1 - slot, hh] = nxt[hh]
        front(zs, slot, False, range(gps))
        back(key_start(i - 1), 1 - slot, range(gps))

    acc_ref[...] = jnp.zeros_like(acc_ref)
    front(scores(key_start(0)), 0, True, range(gps))
    nxt = scores(key_start(1))
    for hh in heads:
        z_ref[1, hh] = nxt[hh]

    def body(t, _):
        step(2 * t + 1, 1)

        @pl.when(2 * t + 2 <= qi)
        def _():
            step(2 * t + 2, 0)

        return 0

    lax.fori_loop(0, (qi + 1) // 2, body, 0)
    back(key_start(qi), qi % 2, range(gps))
    o_ref[...] = acc_ref[...].astype(BF16)


def _sb_attention(p_sb3, tri, blk, gps):
    bsz, seq, _ = p_sb3.shape
    nq = seq // blk
    width = gps * LANES
    kblk0 = BRANCH_WIDTH // width
    n_heads_step = HEADS_PER_GROUP * gps
    return pl.pallas_call(
        functools.partial(_sb_attn_kernel, blk=blk, gps=gps),
        grid=(bsz, N_GROUPS // gps, nq),
        in_specs=[
            pl.BlockSpec((None, blk, width), lambda b, g, i: (b, i, g)),
            pl.BlockSpec((None, seq, width), lambda b, g, i: (b, 0, kblk0 + g)),
            pl.BlockSpec((None, seq, width), lambda b, g, i: (b, 0, 2 * kblk0 + g)),
            pl.BlockSpec((blk, blk), lambda b, g, i: (0, 0)),
        ],
        out_specs=pl.BlockSpec((None, blk, width), lambda b, g, i: (b, i, g)),
        out_shape=jax.ShapeDtypeStruct((bsz, seq, BRANCH_WIDTH), BF16),
        scratch_shapes=[
            pltpu.VMEM((n_heads_step, seq, LANES), BF16),
            pltpu.VMEM((2, n_heads_step, blk, blk), F32),
            pltpu.VMEM((2, n_heads_step, blk, blk), F32),
            pltpu.VMEM((2, blk, width), F32),
            pltpu.VMEM((blk, width), F32),
            pltpu.VMEM((blk, width), F32),
        ],
        compiler_params=pltpu.CompilerParams(
            dimension_semantics=("parallel", "parallel", "arbitrary"),
            vmem_limit_bytes=VMEM_LIMIT),
        name="sb_attention",
    )(p_sb3, p_sb3, p_sb3, tri)


def _rwkv_kernel(xs_ref, dbase_ref, dup_ref, ibase_ref, iup_ref, gup_ref, keyk_ref,
                 keya_ref, bonus_ref, gnw_ref, gnb_ref, seg_ref, tri_ref, o_ref,
                 h_ref, *, n_sub):
    c = CHUNK
    bw = BRANCH_WIDTH

    @pl.when(pl.program_id(1) == 0)
    def _():
        h_ref[...] = jnp.zeros_like(h_ref)

    xs = xs_ref[...]
    r = xs[:, 0:bw]
    k = xs[:, bw:2 * bw]
    v = xs[:, 2 * bw:3 * bw]
    o0 = 3 * bw
    w_lo = xs[:, o0:o0 + LORA_W_PAD]
    a_lo = xs[:, o0 + LORA_W_PAD:o0 + LORA_W_PAD + LORA_A_PAD]
    g_lo = xs[:, o0 + LORA_W_PAD + LORA_A_PAD:]
    seg = seg_ref[...]

    def seg_sum(x, passes):
        parts = []
        for g in range(N_GROUPS):
            xg = x[:, g * LANES:(g + 1) * LANES]
            parts.append(_dot2l(xg, seg) if passes == 2 else _dot1(xg, seg))
        return jnp.concatenate(parts, axis=-1)


    dw = dbase_ref[...] + _dot3(jnp.tanh(w_lo), dup_ref[...])
    w_log = -_softplus(-dw) - 0.5
    lw = -jnp.exp(w_log)
    iclr = _sigmoid(ibase_ref[...] + _dot1(a_lo, iup_ref[...]))
    gate = _dot1(_sigmoid(g_lo), gup_ref[...])
    kk0 = k * keyk_ref[...]
    kk = kk0 / jnp.maximum(jnp.sqrt(seg_sum(kk0 * kk0, 1)), 1e-12)
    kmod = k * (1.0 + (iclr - 1.0) * keya_ref[...])
    a = -kk
    b = kk * iclr
    bonus = seg_sum(r * kmod * bonus_ref[...], 1) * v

    tri = tri_ref[...]
    row = lax.broadcasted_iota(jnp.int32, (LANES, LANES), 0)
    col = lax.broadcasted_iota(jnp.int32, (LANES, LANES), 1)
    same_head = (row // c) == (col // c)
    strict_lo = jnp.logical_and(same_head, (col % c) < (row % c))
    incl_lo = jnp.logical_and(same_head, (col % c) <= (row % c))
    eye = row == col
    lane = lax.broadcasted_iota(jnp.int32, (c, LANES), 1)
    head_masks = [(lane // HEAD_DIM) == hh for hh in range(HEADS_PER_GROUP)]

    def stack(x):
        return jnp.concatenate([jnp.where(m, x, 0.0) for m in head_masks], axis=0)

    chains = [(s, g) for s in range(n_sub) for g in range(N_GROUPS)]
    ops = {}
    for s in range(n_sub):
        sl = slice(s * c, (s + 1) * c)
        lw_s = lw[sl]
        cum = _dot2r(tri, lw_s)
        tot = cum[c - 1:c, :]
        e_cum = jnp.exp(cum)
        e_prev = jnp.exp(cum - lw_s)
        e_inv = jnp.exp(-cum)
        e_rem = jnp.exp(tot - cum)
        p_tot = jnp.exp(tot)
        full = dict(a=a[sl] * e_prev, r=r[sl] * e_cum, b=b[sl] * e_inv, k=kmod[sl] * e_inv,
                    bh=b[sl] * e_rem, kh=kmod[sl] * e_rem, v=v[sl])
        for g in range(N_GROUPS):
            ls = slice(g * LANES, (g + 1) * LANES)
            st = {name: stack(val[:, ls]) for name, val in full.items()}
            ops[s, g] = dict(
                a=st["a"].astype(BF16), b=st["b"].astype(BF16), k=st["k"].astype(BF16),
                v=st["v"].astype(BF16), r=st["r"], bh_t=st["bh"].T.astype(BF16),
                kh_t=st["kh"].T.astype(BF16), p_tot=p_tot[:, ls])

    l_pow, t_inv, l_ak, m_rb, m_rk = {}, {}, {}, {}, {}
    for ch in chains:
        o = ops[ch]
        sc = _dot_t(jnp.concatenate([o["a"], o["r"].astype(BF16)], axis=0),
                    jnp.concatenate([o["b"], o["k"]], axis=0))
        l_ab = jnp.where(strict_lo, sc[:2 * c, :2 * c], 0.0)
        l_ak[ch] = jnp.where(strict_lo, sc[:2 * c, 2 * c:], 0.0).astype(BF16)
        m_rb[ch] = jnp.where(incl_lo, sc[2 * c:, :2 * c], 0.0).astype(BF16)
        m_rk[ch] = jnp.where(incl_lo, sc[2 * c:, 2 * c:], 0.0).astype(BF16)
        l_pow[ch] = l_ab
        t_inv[ch] = jnp.where(eye, 1.0, l_ab)
    for _ in range(5):
        for ch in chains:
            lp = l_pow[ch].astype(BF16)
            l_pow[ch] = _dot(lp, lp)
        for ch in chains:
            t_inv[ch] = t_inv[ch] + _dot1(t_inv[ch], l_pow[ch])
    rh2, yl2, m_mat, n_mat, p_col = {}, {}, {}, {}, {}
    x_loc = {ch: _dot(l_ak[ch], ops[ch]["v"]) for ch in chains}
    zw = {ch: _dot(t_inv[ch].astype(BF16),
                   jnp.concatenate([ops[ch]["a"], x_loc[ch].astype(BF16)], axis=1))
          for ch in chains}
    for ch in chains:
        o = ops[ch]
        ap2 = zw[ch][:, :LANES].astype(BF16)
        wt2 = zw[ch][:, LANES:].astype(BF16)
        rh2[ch] = o["r"] + _dot(m_rb[ch], ap2)
        yl2[ch] = _dot(m_rb[ch], wt2) + _dot(m_rk[ch], o["v"])
        m_mat[ch] = _dot(o["bh_t"], ap2).astype(BF16)
        p_col[ch] = jnp.sum(jnp.where(eye, o["p_tot"], 0.0), axis=1, keepdims=True)
        n_mat[ch] = _dot(o["bh_t"], wt2) + _dot(o["kh_t"], o["v"])

    ys = []
    h = [h_ref[g] for g in range(N_GROUPS)]
    for s in range(n_sub):
        y_groups = []
        for g in range(N_GROUPS):
            y2 = _dot1(rh2[s, g], h[g]) + yl2[s, g]
            h[g] = p_col[s, g] * h[g] + _dot1(m_mat[s, g], h[g]) + n_mat[s, g]
            y_groups.append(y2[:c] + y2[c:])
        ys.append(jnp.concatenate(y_groups, axis=-1))
    for g in range(N_GROUPS):
        h_ref[g] = h[g]
    y = jnp.concatenate(ys, axis=0) if n_sub > 1 else ys[0]

    inv_n = 1.0 / HEAD_DIM
    mu = seg_sum(y, 2) * inv_n
    d = y - mu
    var = seg_sum(d * d, 1) * inv_n
    yn = d * lax.rsqrt(var + GN_EPS) * gnw_ref[...] + gnb_ref[...]
    o_ref[...] = ((yn + bonus) * gate).astype(BF16)


def _rwkv_mix(xs3, dbase, dup, ibase, iup, gup, keyk, keya, bonus, gnw, gnb, seg, tri, tc):
    bsz, seq, ncol = xs3.shape
    const = lambda b, i: (0, 0)
    vec = pl.BlockSpec((1, BRANCH_WIDTH), const)
    return pl.pallas_call(
        functools.partial(_rwkv_kernel, n_sub=tc // CHUNK),
        grid=(bsz, seq // tc),
        in_specs=[
            pl.BlockSpec((None, tc, ncol), lambda b, i: (b, i, 0)),
            vec,
            pl.BlockSpec(dup.shape, const),
            vec,
            pl.BlockSpec(iup.shape, const),
            pl.BlockSpec(gup.shape, const),
            vec, vec, vec, vec, vec,
            pl.BlockSpec(seg.shape, const),
            pl.BlockSpec(tri.shape, const),
        ],
        out_specs=pl.BlockSpec((None, tc, BRANCH_WIDTH), lambda b, i: (b, i, 0)),
        out_shape=jax.ShapeDtypeStruct((bsz, seq, BRANCH_WIDTH), BF16),
        scratch_shapes=[pltpu.VMEM((N_GROUPS, LANES, LANES), F32)],
        compiler_params=pltpu.CompilerParams(
            dimension_semantics=("parallel", "arbitrary"), vmem_limit_bytes=VMEM_LIMIT),
        name="rwkv_mix",
    )(xs3, dbase, dup, ibase, iup, gup, keyk, keya, bonus, gnw, gnb, seg, tri)


def _merge_kernel(x_ref, osb_ref, orw_ref, pg_ref, bg_ref, wb0_ref, wb1_ref, wout_ref,
                  g2_ref, x1_ref, h2_ref):
    d = x_ref.shape[1]
    up0 = _dot(osb_ref[...], wb0_ref[...])
    up1 = _dot(orw_ref[...], wb1_ref[...])
    gates = _sigmoid(pg_ref[...].astype(F32) + bg_ref[...])
    mixed = gates[:, :d] * up0 + gates[:, d:] * up1
    x1 = x_ref[...] + _dot(mixed.astype(BF16), wout_ref[...])
    x1_ref[...] = x1
    h2 = x1 * lax.rsqrt(jnp.mean(x1 * x1, axis=-1, keepdims=True) + NORM_EPS) * g2_ref[...]
    h2_ref[...] = h2.astype(BF16)


def _merge(x2, o_sb, o_rw, p_gate, bg, wb0, wb1, w_out, g2, tm):
    t, d = x2.shape
    const = lambda i: (0, 0)
    row = lambda i: (i, 0)
    return pl.pallas_call(
        _merge_kernel,
        grid=(t // tm,),
        in_specs=[
            pl.BlockSpec((tm, d), row),
            pl.BlockSpec((tm, BRANCH_WIDTH), row),
            pl.BlockSpec((tm, BRANCH_WIDTH), row),
            pl.BlockSpec((tm, 2 * d), row),
            pl.BlockSpec((1, 2 * d), const),
            pl.BlockSpec(wb0.shape, const),
            pl.BlockSpec(wb1.shape, const),
            pl.BlockSpec(w_out.shape, const),
            pl.BlockSpec((1, d), const),
        ],
        out_specs=[pl.BlockSpec((tm, d), row), pl.BlockSpec((tm, d), row)],
        out_shape=[jax.ShapeDtypeStruct((t, d), F32), jax.ShapeDtypeStruct((t, d), BF16)],
        compiler_params=pltpu.CompilerParams(
            dimension_semantics=("parallel",), vmem_limit_bytes=VMEM_LIMIT),
        name="merge_outproj",
    )(x2, o_sb, o_rw, p_gate, bg, wb0, wb1, w_out, g2)


def _ffn_kernel(h2_ref, x1_ref, wup_ref, cw_ref, cb_ref, wdown_ref, o_ref, carry_ref,
                *, tiles_per_seq, d_ff, nc, sub):
    i = pl.program_id(0)
    first_tile = (i % tiles_per_seq) == 0
    tm = h2_ref.shape[0]
    n_sub = tm // sub
    n_ch = d_ff // nc
    row = lax.broadcasted_iota(jnp.int32, (sub, nc), 0)
    rows = [slice(s * sub, (s + 1) * sub) for s in range(n_sub)]
    h2s = [h2_ref[r, :] for r in rows]

    def conv(u, col0, seq_start):
        cs = slice(col0, col0 + nc)
        old = carry_ref[:, cs]
        if seq_start is not None:
            old = jnp.where(seq_start, 0.0, old)
        p1 = jnp.where(row == 0, old[7:8, :], pltpu.roll(u, 1, 0))
        p2 = jnp.where(row == 0, old[6:7, :],
                       jnp.where(row == 1, old[7:8, :], pltpu.roll(u, 2, 0)))
        carry_ref[:, cs] = u[sub - 8:, :]
        cw = cw_ref[:, cs]
        return cw[0:1] * p2 + cw[1:2] * p1 + cw[2:3] * u + cb_ref[:, cs]

    def up(item):
        s, ch = item
        return (_dot(h2s[s], wup_ref[:, ch * nc:(ch + 1) * nc]),
                _dot(h2s[s], wup_ref[:, d_ff + ch * nc:d_ff + (ch + 1) * nc]))

    items = [(s, ch) for s in range(n_sub) for ch in range(n_ch)]
    accs = [x1_ref[r, :] for r in rows]
    ahead = 3
    pending = [up(it) for it in items[:ahead]]
    for k, (s, ch) in enumerate(items):
        ug, uv = pending.pop(0)
        if k + ahead < len(items):
            pending.append(up(items[k + ahead]))
        seq_start = first_tile if s == 0 else None
        cg = conv(ug, ch * nc, seq_start)
        cv = conv(uv, d_ff + ch * nc, seq_start)
        act = cg * _sigmoid(cg) * cv
        accs[s] = accs[s] + _dot(act.astype(BF16), wdown_ref[ch * nc:(ch + 1) * nc, :])
    for s, r in enumerate(rows):
        o_ref[r, :] = accs[s]


def _ffn(h2, x1, w_up, conv_w, conv_b, w_down, seq, tm, nc, sub):
    t, d = x1.shape
    d_ff = w_down.shape[0]
    const = lambda i: (0, 0)
    row = lambda i: (i, 0)
    return pl.pallas_call(
        functools.partial(_ffn_kernel, tiles_per_seq=seq // tm, d_ff=d_ff, nc=nc, sub=sub),
        grid=(t // tm,),
        in_specs=[
            pl.BlockSpec((tm, d), row),
            pl.BlockSpec((tm, d), row),
            pl.BlockSpec(w_up.shape, const),
            pl.BlockSpec(conv_w.shape, const),
            pl.BlockSpec(conv_b.shape, const),
            pl.BlockSpec(w_down.shape, const),
        ],
        out_specs=pl.BlockSpec((tm, d), row),
        out_shape=jax.ShapeDtypeStruct((t, d), F32),
        scratch_shapes=[pltpu.VMEM((8, 2 * d_ff), F32)],
        compiler_params=pltpu.CompilerParams(
            dimension_semantics=("arbitrary",), vmem_limit_bytes=VMEM_LIMIT),
        name="ffn",
    )(h2, x1, w_up, conv_w, conv_b, w_down)


def _pad_cols(w, n):
    return jnp.pad(w, ((0, 0), (0, n - w.shape[1])))


def _pad_rows(w, n):
    return jnp.pad(w, ((0, n - w.shape[0]), (0, 0)))


def _layer(x, attn_norm_g, w_in, q_norm_g, k_norm_g, rwkv_shift_mu, decay_base, decay_up,
           iclr_base, iclr_up, out_gate_up, key_k, key_a, bonus_rk, group_norm_w,
           group_norm_b, branch_gate_b, w_branch, w_out, ffn_norm_g, w_ffn_up,
           ffn_conv_w, ffn_conv_b, w_ffn_down):
    bsz, seq, d = x.shape
    t = bsz * seq
    bw = BRANCH_WIDTH
    sb_cols = 3 * bw
    rw_cols = 3 * bw + DECAY_LORA + ICLR_LORA + GATE_LORA
    x2 = x.reshape(t, d)

    w_sb = w_in[:, :sb_cols].astype(BF16)
    w_rw_raw = w_in[:, sb_cols:sb_cols + rw_cols]
    mu_raw = rwkv_shift_mu.reshape(1, rw_cols)

    def regroup(m):
        o = 3 * bw
        return jnp.concatenate([
            m[:, :o],
            _pad_cols(m[:, o:o + DECAY_LORA], LORA_W_PAD),
            _pad_cols(m[:, o + DECAY_LORA:o + DECAY_LORA + ICLR_LORA], LORA_A_PAD),
            _pad_cols(m[:, o + DECAY_LORA + ICLR_LORA:], LORA_G_PAD),
        ], axis=1)

    w_rw = regroup(w_rw_raw).astype(BF16)
    mu = regroup(mu_raw)
    w_gate = w_in[:, sb_cols + rw_cols:].astype(BF16)

    qg_pair = jnp.tile(q_norm_g.reshape(1, HEAD_DIM), (1, HEADS_PER_GROUP))
    kg_pair = jnp.tile(k_norm_g.reshape(1, HEAD_DIM), (1, HEADS_PER_GROUP))
    p_sb, xs, p_gate = _in_projection(
        x2, attn_norm_g.reshape(1, d), w_sb, w_rw, w_gate, mu, qg_pair, kg_pair, seq,
        tm=512, sub=256)

    blk = 256
    tri_sb = (jnp.arange(blk)[:, None] > jnp.arange(blk)[None, :]).astype(BF16)
    o_sb = _sb_attention(p_sb.reshape(bsz, seq, sb_cols), tri_sb, blk, gps=4)

    head_id = jnp.arange(LANES) // HEAD_DIM
    seg = (head_id[:, None] == head_id[None, :]).astype(BF16)
    tri_rw = (jnp.arange(CHUNK)[:, None] >= jnp.arange(CHUNK)[None, :]).astype(BF16)
    vec = lambda p: p.reshape(1, bw)
    o_rw = _rwkv_mix(
        xs.reshape(bsz, seq, RW_COLS_PAD), vec(decay_base), _pad_rows(decay_up, LORA_W_PAD),
        vec(iclr_base), _pad_rows(iclr_up, LORA_A_PAD), _pad_rows(out_gate_up, LORA_G_PAD),
        vec(key_k), vec(key_a), vec(bonus_rk), vec(group_norm_w), vec(group_norm_b),
        seg, tri_rw, tc=512)

    x1, h2 = _merge(
        x2, o_sb.reshape(t, bw), o_rw.reshape(t, bw), p_gate,
        branch_gate_b.reshape(1, 2 * d), w_branch[0].astype(BF16), w_branch[1].astype(BF16),
        w_out.astype(BF16), ffn_norm_g.reshape(1, d), tm=512)

    out = _ffn(h2, x1, w_ffn_up.astype(BF16), ffn_conv_w, ffn_conv_b.reshape(1, -1),
               w_ffn_down.astype(BF16), seq, tm=512, nc=256, sub=256)
    return out.reshape(bsz, seq, d)


def kernel(x, attn_norm_g, w_in, q_norm_g, k_norm_g, rwkv_shift_mu, decay_base, decay_up,
           iclr_base, iclr_up, out_gate_up, key_k, key_a, bonus_rk, group_norm_w,
           group_norm_b, branch_gate_b, w_branch, w_out, ffn_norm_g, w_ffn_up, ffn_conv_w,
           ffn_conv_b, w_ffn_down):
    for layer in range(attn_norm_g.shape[0]):
        x = _layer(
            x, attn_norm_g[layer], w_in[layer], q_norm_g[layer], k_norm_g[layer],
            rwkv_shift_mu[layer], decay_base[layer], decay_up[layer], iclr_base[layer],
            iclr_up[layer], out_gate_up[layer], key_k[layer], key_a[layer], bonus_rk[layer],
            group_norm_w[layer], group_norm_b[layer], branch_gate_b[layer], w_branch[layer],
            w_out[layer], ffn_norm_g[layer], w_ffn_up[layer], ffn_conv_w[layer],
            ffn_conv_b[layer], w_ffn_down[layer])
    return x
```

```python
import functools

import jax
import jax.numpy as jnp
from jax import lax
from jax.experimental import pallas as pl
from jax.experimental.pallas import tpu as pltpu

F32 = jnp.float32
BF16 = jnp.bfloat16

HEAD_DIM = 64
N_HEADS = 8
BRANCH_WIDTH = N_HEADS * HEAD_DIM
LANES = 128
HEADS_PER_GROUP = LANES // HEAD_DIM
N_GROUPS = BRANCH_WIDTH // LANES
DECAY_LORA = 64
ICLR_LORA = 64
GATE_LORA = 160
LORA_W_PAD = 128
LORA_A_PAD = 128
LORA_G_PAD = 256
RW_COLS_PAD = 3 * BRANCH_WIDTH + LORA_W_PAD + LORA_A_PAD + LORA_G_PAD
CONV_WIDTH = 3
NORM_EPS = 1e-6
GN_EPS = 64e-5
LOG2E = 1.4426950408889634
MASKED_LOG2 = -1e30
CHUNK = 64
VMEM_LIMIT = 56 * 1024 * 1024


def _dot(a, b):
    return jnp.dot(a, b, preferred_element_type=F32)


def _dot_t(a, b):
    return lax.dot_general(a, b, (((1,), (1,)), ((), ())), preferred_element_type=F32)


def _split(a):
    hi = a.astype(BF16)
    lo = (a - hi.astype(F32)).astype(BF16)
    return hi, lo


def _dot3(a, b):
    ah, al = _split(a)
    bh, bl = _split(b)
    return _dot(ah, bh) + _dot(ah, bl) + _dot(al, bh)


def _dot1(a, b):
    return _dot(a.astype(BF16), b.astype(BF16))


def _dot2l(a, b_bf16):
    ah, al = _split(a)
    return _dot(ah, b_bf16) + _dot(al, b_bf16)


def _dot2r(a_bf16, b):
    bh, bl = _split(b)
    return _dot(a_bf16, bh) + _dot(a_bf16, bl)


def _sigmoid(x):
    return 1.0 / (1.0 + jnp.exp(-x))


def _softplus(x):
    return jnp.maximum(x, 0.0) + jnp.log(1.0 + jnp.exp(-jnp.abs(x)))


def _inproj_kernel(x_ref, g_ref, wsb_ref, wrw_ref, wgate_ref, mu_ref, qg_ref, kg_ref,
                   psb_ref, xs_ref, pgate_ref, carry_ref, *, tiles_per_seq, sub):
    i = pl.program_id(0)
    first_tile = (i % tiles_per_seq) == 0
    row = lax.broadcasted_iota(jnp.int32, (sub, xs_ref.shape[1]), 0)
    lane_lo = lax.broadcasted_iota(jnp.int32, (sub, LANES), 1) < HEAD_DIM
    for s in range(x_ref.shape[0] // sub):
        r = slice(s * sub, (s + 1) * sub)
        x = x_ref[r, :]
        h = x * lax.rsqrt(jnp.mean(x * x, axis=-1, keepdims=True) + NORM_EPS) * g_ref[...]
        hb = h.astype(BF16)
        p_sb = _dot(hb, wsb_ref[...])
        for j in range(BRANCH_WIDTH // LANES):
            for base, gain, scale in ((0, qg_ref, HEAD_DIM ** -0.5 * LOG2E), (BRANCH_WIDTH, kg_ref, 1.0)):
                cols = slice(base + j * LANES, base + (j + 1) * LANES)
                normed = _head_rmsnorm(p_sb[:, cols], gain[...], lane_lo)
                psb_ref[r, cols] = (normed * scale).astype(BF16)
        psb_ref[r, 2 * BRANCH_WIDTH:] = p_sb[:, 2 * BRANCH_WIDTH:].astype(BF16)
        pgate_ref[r, :] = _dot(hb, wgate_ref[...]).astype(BF16)
        p = _dot(hb, wrw_ref[...])
        prev_last = carry_ref[7:8, :]
        if s == 0:
            prev_last = jnp.where(first_tile, 0.0, prev_last)
        prev = jnp.where(row == 0, prev_last, pltpu.roll(p, 1, 0))
        xs_ref[r, :] = p + (prev - p) * mu_ref[...]
        carry_ref[...] = p[sub - 8:, :]


def _in_projection(x2, g, w_sb, w_rw, w_gate, mu, qg_pair, kg_pair, seq, tm, sub):
    t, d = x2.shape
    n_sb, n_rw, n_gate = w_sb.shape[1], w_rw.shape[1], w_gate.shape[1]
    const = lambda i: (0, 0)
    row = lambda i: (i, 0)
    return pl.pallas_call(
        functools.partial(_inproj_kernel, tiles_per_seq=seq // tm, sub=sub),
        grid=(t // tm,),
        in_specs=[
            pl.BlockSpec((tm, d), row),
            pl.BlockSpec((1, d), const),
            pl.BlockSpec((d, n_sb), const),
            pl.BlockSpec((d, n_rw), const),
            pl.BlockSpec((d, n_gate), const),
            pl.BlockSpec((1, n_rw), const),
            pl.BlockSpec((1, LANES), const),
            pl.BlockSpec((1, LANES), const),
        ],
        out_specs=[
            pl.BlockSpec((tm, n_sb), row),
            pl.BlockSpec((tm, n_rw), row),
            pl.BlockSpec((tm, n_gate), row),
        ],
        out_shape=[
            jax.ShapeDtypeStruct((t, n_sb), BF16),
            jax.ShapeDtypeStruct((t, n_rw), F32),
            jax.ShapeDtypeStruct((t, n_gate), BF16),
        ],
        scratch_shapes=[pltpu.VMEM((8, n_rw), F32)],
        compiler_params=pltpu.CompilerParams(
            dimension_semantics=("arbitrary",), vmem_limit_bytes=VMEM_LIMIT),
        name="in_projection",
    )(x2, g, w_sb, w_rw, w_gate, mu, qg_pair, kg_pair)


def _head_rmsnorm(x, g_pair, lane_lo):
    sq = x * x
    s_lo = jnp.sum(jnp.where(lane_lo, sq, 0.0), axis=-1, keepdims=True)
    s_all = jnp.sum(sq, axis=-1, keepdims=True)
    ms = jnp.where(lane_lo, s_lo, s_all - s_lo) * (1.0 / HEAD_DIM)
    return x * lax.rsqrt(ms + NORM_EPS) * g_pair


def _sb_attn_kernel(q_ref, k_ref, v_ref, tri_ref, o_ref,
                    vm_ref, z_ref, x_ref, keep_ref, acc_ref, carry_ref, *, blk, gps):
    qi = pl.program_id(2)
    heads = range(HEADS_PER_GROUP * gps)
    group_lanes = [slice(g * LANES, (g + 1) * LANES) for g in range(gps)]

    @pl.when(qi == 0)
    def _():
        lane_lo_k = lax.broadcasted_iota(jnp.int32, (k_ref.shape[0], LANES), 1) < HEAD_DIM
        for g, gl in enumerate(group_lanes):
            v = v_ref[:, gl]
            zero = jnp.zeros_like(v)
            vm_ref[2 * g] = jnp.where(lane_lo_k, v, zero)
            vm_ref[2 * g + 1] = jnp.where(lane_lo_k, zero, v)

    lane_lo = lax.broadcasted_iota(jnp.int32, (blk, LANES), 1) < HEAD_DIM
    q_heads = []
    for gl in group_lanes:
        qn = q_ref[:, gl]
        zero = jnp.zeros_like(qn)
        q_heads += [jnp.where(lane_lo, qn, zero), jnp.where(lane_lo, zero, qn)]
    tri = tri_ref[...]
    t_idx = lax.broadcasted_iota(jnp.int32, (blk, blk), 0)
    s_idx = lax.broadcasted_iota(jnp.int32, (blk, blk), 1)
    strict = s_idx < t_idx
    row_parts = 2
    lane_lo_half = lax.broadcasted_iota(jnp.int32, (blk // row_parts, LANES), 1) < HEAD_DIM

    def key_start(i):
        return pl.multiple_of(jnp.maximum(qi - i, 0) * blk, blk)

    def scores(start):
        kbs = [k_ref[pl.ds(start, blk), gl] for gl in group_lanes]
        return [_dot_t(q_heads[hh], kbs[hh // HEADS_PER_GROUP]) for hh in heads]

    def front(zs, slot_w, diag, groups):
        half = blk // row_parts
        hs = [hh for g in groups for hh in (2 * g, 2 * g + 1)]
        for rows in [slice(p * half, (p + 1) * half) for p in range(row_parts)]:
            lss, lks = {}, {}
            for hh in hs:
                z = zs[hh][rows]
                sp = (jnp.log(1.0 + jnp.exp2(-jnp.abs(z.astype(BF16)))) * LOG2E).astype(F32)
                ls = jnp.minimum(z, 0.0) - sp
                lk = ls - z
                lss[hh] = ls
                lks[hh] = jnp.where(strict[rows], lk, 0.0) if diag else lk
            afters = {hh: _dot1(lks[hh], tri) for hh in hs}
            sums = {hh: jnp.sum(lks[hh], axis=-1, keepdims=True) for hh in hs}
            for g in groups:
                gl = group_lanes[g]
                sum_pair = jnp.where(lane_lo_half, sums[2 * g], sums[2 * g + 1])
                if diag:
                    keep_ref[slot_w, rows, gl] = jnp.ones((half, LANES), F32)
                    carry_ref[rows, gl] = sum_pair
                else:
                    carry = carry_ref[rows, gl]
                    keep_ref[slot_w, rows, gl] = jnp.exp2(carry)
                    carry_ref[rows, gl] = carry + sum_pair
            for hh in hs:
                x = lss[hh] + afters[hh]
                x_ref[slot_w, hh, rows, :] = (
                    jnp.where(strict[rows], x, MASKED_LOG2) if diag else x)

    def back(start, slot_r, groups):
        for g in groups:
            gl = group_lanes[g]
            ws = {hh: jnp.exp2(x_ref[slot_r, hh]).astype(BF16) for hh in (2 * g, 2 * g + 1)}
            pv = sum(_dot(ws[hh], vm_ref[hh, pl.ds(start, blk), :])
                     for hh in (2 * g, 2 * g + 1))
            acc_ref[:, gl] = acc_ref[:, gl] + keep_ref[slot_r, :, gl] * pv

    def step(i, slot):
        zs = [z_ref[slot, hh] for hh in heads]
        nxt = scores(key_start(i + 1))
        for hh in heads:
            z_ref[1 - slot, hh] = nxt[hh]
        front(zs, slot, False, range(gps))
        back(key_start(i - 1), 1 - slot, range(gps))

    acc_ref[...] = jnp.zeros_like(acc_ref)
    front(scores(key_start(0)), 0, True, range(gps))
    nxt = scores(key_start(1))
    for hh in heads:
        z_ref[1, hh] = nxt[hh]

    def body(t, _):
        step(2 * t + 1, 1)

        @pl.when(2 * t + 2 <= qi)
        def _():
            step(2 * t + 2, 0)

        return 0

    lax.fori_loop(0, (qi + 1) // 2, body, 0)
    back(key_start(qi), qi % 2, range(gps))
    o_ref[...] = acc_ref[...].astype(BF16)


def _sb_attention(p_sb3, tri, blk, gps):
    bsz, seq, _ = p_sb3.shape
    nq = seq // blk
    width = gps * LANES
    kblk0 = BRANCH_WIDTH // width
    n_heads_step = HEADS_PER_GROUP * gps
    return pl.pallas_call(
        functools.partial(_sb_attn_kernel, blk=blk, gps=gps),
        grid=(bsz, N_GROUPS // gps, nq),
        in_specs=[
            pl.BlockSpec((None, blk, width), lambda b, g, i: (b, i, g)),
            pl.BlockSpec((None, seq, width), lambda b, g, i: (b, 0, kblk0 + g)),
            pl.BlockSpec((None, seq, width), lambda b, g, i: (b, 0, 2 * kblk0 + g)),
            pl.BlockSpec((blk, blk), lambda b, g, i: (0, 0)),
        ],
        out_specs=pl.BlockSpec((None, blk, width), lambda b, g, i: (b, i, g)),
        out_shape=jax.ShapeDtypeStruct((bsz, seq, BRANCH_WIDTH), BF16),
        scratch_shapes=[
            pltpu.VMEM((n_heads_step, seq, LANES), BF16),
            pltpu.VMEM((2, n_heads_step, blk, blk), F32),
            pltpu.VMEM((2, n_heads_step, blk, blk), F32),
            pltpu.VMEM((2, blk, width), F32),
            pltpu.VMEM((blk, width), F32),
            pltpu.VMEM((blk, width), F32),
        ],
        compiler_params=pltpu.CompilerParams(
            dimension_semantics=("parallel", "parallel", "arbitrary"),
            vmem_limit_bytes=VMEM_LIMIT),
        name="sb_attention",
    )(p_sb3, p_sb3, p_sb3, tri)


def _rwkv_kernel(xs_ref, dbase_ref, dup_ref, ibase_ref, iup_ref, gup_ref, keyk_ref,
                 keya_ref, bonus_ref, gnw_ref, gnb_ref, seg_ref, tri_ref, o_ref,
                 h_ref, *, n_sub):
    c = CHUNK
    bw = BRANCH_WIDTH

    @pl.when(pl.program_id(1) == 0)
    def _():
        h_ref[...] = jnp.zeros_like(h_ref)

    xs = xs_ref[...]
    r = xs[:, 0:bw]
    k = xs[:, bw:2 * bw]
    v = xs[:, 2 * bw:3 * bw]
    o0 = 3 * bw
    w_lo = xs[:, o0:o0 + LORA_W_PAD]
    a_lo = xs[:, o0 + LORA_W_PAD:o0 + LORA_W_PAD + LORA_A_PAD]
    g_lo = xs[:, o0 + LORA_W_PAD + LORA_A_PAD:]
    seg = seg_ref[...]

    def seg_sum(x, passes):
        parts = []
        for g in range(N_GROUPS):
            xg = x[:, g * LANES:(g + 1) * LANES]
            parts.append(_dot2l(xg, seg) if passes == 2 else _dot1(xg, seg))
        return jnp.concatenate(parts, axis=-1)


    dw = dbase_ref[...] + _dot3(jnp.tanh(w_lo), dup_ref[...])
    w_log = -_softplus(-dw) - 0.5
    lw = -jnp.exp(w_log)
    iclr = _sigmoid(ibase_ref[...] + _dot1(a_lo, iup_ref[...]))
    gate = _dot1(_sigmoid(g_lo), gup_ref[...])
    kk0 = k * keyk_ref[...]
    kk = kk0 / jnp.maximum(jnp.sqrt(seg_sum(kk0 * kk0, 1)), 1e-12)
    kmod = k * (1.0 + (iclr - 1.0) * keya_ref[...])
    a = -kk
    b = kk * iclr
    bonus = seg_sum(r * kmod * bonus_ref[...], 1) * v

    tri = tri_ref[...]
    row = lax.broadcasted_iota(jnp.int32, (LANES, LANES), 0)
    col = lax.broadcasted_iota(jnp.int32, (LANES, LANES), 1)
    same_head = (row // c) == (col // c)
    strict_lo = jnp.logical_and(same_head, (col % c) < (row % c))
    incl_lo = jnp.logical_and(same_head, (col % c) <= (row % c))
    eye = row == col
    lane = lax.broadcasted_iota(jnp.int32, (c, LANES), 1)
    head_masks = [(lane // HEAD_DIM) == hh for hh in range(HEADS_PER_GROUP)]

    def stack(x):
        return jnp.concatenate([jnp.where(m, x, 0.0) for m in head_masks], axis=0)

    chains = [(s, g) for s in range(n_sub) for g in range(N_GROUPS)]
    ops = {}
    for s in range(n_sub):
        sl = slice(s * c, (s + 1) * c)
        lw_s = lw[sl]
        cum = _dot2r(tri, lw_s)
        tot = cum[c - 1:c, :]
        e_cum = jnp.exp(cum)
        e_prev = jnp.exp(cum - lw_s)
        e_inv = jnp.exp(-cum)
        e_rem = jnp.exp(tot - cum)
        p_tot = jnp.exp(tot)
        full = dict(a=a[sl] * e_prev, r=r[sl] * e_cum, b=b[sl] * e_inv, k=kmod[sl] * e_inv,
                    bh=b[sl] * e_rem, kh=kmod[sl] * e_rem, v=v[sl])
        for g in range(N_GROUPS):
            ls = slice(g * LANES, (g + 1) * LANES)
            st = {name: stack(val[:, ls]) for name, val in full.items()}
            ops[s, g] = dict(
                a=st["a"].astype(BF16), b=st["b"].astype(BF16), k=st["k"].astype(BF16),
                v=st["v"].astype(BF16), r=st["r"], bh_t=st["bh"].T.astype(BF16),
                kh_t=st["kh"].T.astype(BF16), p_tot=p_tot[:, ls])

    l_pow, t_inv, l_ak, m_rb, m_rk = {}, {}, {}, {}, {}
    for ch in chains:
        o = ops[ch]
        sc = _dot_t(jnp.concatenate([o["a"], o["r"].astype(BF16)], axis=0),
                    jnp.concatenate([o["b"], o["k"]], axis=0))
        l_ab = jnp.where(strict_lo, sc[:2 * c, :2 * c], 0.0)
        l_ak[ch] = jnp.where(strict_lo, sc[:2 * c, 2 * c:], 0.0).astype(BF16)
        m_rb[ch] = jnp.where(incl_lo, sc[2 * c:, :2 * c], 0.0).astype(BF16)
        m_rk[ch] = jnp.where(incl_lo, sc[2 * c:, 2 * c:], 0.0).astype(BF16)
        l_pow[ch] = l_ab
        t_inv[ch] = jnp.where(eye, 1.0, l_ab)
    for _ in range(5):
        for ch in chains:
            lp = l_pow[ch].astype(BF16)
            l_pow[ch] = _dot(lp, lp)
        for ch in chains:
            t_inv[ch] = t_inv[ch] + _dot1(t_inv[ch], l_pow[ch])
    rh2, yl2, m_mat, n_mat, p_col = {}, {}, {}, {}, {}
    x_loc = {ch: _dot(l_ak[ch], ops[ch]["v"]) for ch in chains}
    zw = {ch: _dot(t_inv[ch].astype(BF16),
                   jnp.concatenate([ops[ch]["a"], x_loc[ch].astype(BF16)], axis=1))
          for ch in chains}
    for ch in chains:
        o = ops[ch]
        ap2 = zw[ch][:, :LANES].astype(BF16)
        wt2 = zw[ch][:, LANES:].astype(BF16)
        rh2[ch] = o["r"] + _dot(m_rb[ch], ap2)
        yl2[ch] = _dot(m_rb[ch], wt2) + _dot(m_rk[ch], o["v"])
        m_mat[ch] = _dot(o["bh_t"], ap2).astype(BF16)
        p_col[ch] = jnp.sum(jnp.where(eye, o["p_tot"], 0.0), axis=1, keepdims=True)
        n_mat[ch] = _dot(o["bh_t"], wt2) + _dot(o["kh_t"], o["v"])

    ys = []
    h = [h_ref[g] for g in range(N_GROUPS)]
    for s in range(n_sub):
        y_groups = []
        for g in range(N_GROUPS):
            y2 = _dot1(rh2[s, g], h[g]) + yl2[s, g]
            h[g] = p_col[s, g] * h[g] + _dot1(m_mat[s, g], h[g]) + n_mat[s, g]
            y_groups.append(y2[:c] + y2[c:])
        ys.append(jnp.concatenate(y_groups, axis=-1))
    for g in range(N_GROUPS):
        h_ref[g] = h[g]
    y = jnp.concatenate(ys, axis=0) if n_sub > 1 else ys[0]

    inv_n = 1.0 / HEAD_DIM
    mu = seg_sum(y, 2) * inv_n
    d = y - mu
    var = seg_sum(d * d, 1) * inv_n
    yn = d * lax.rsqrt(var + GN_EPS) * gnw_ref[...] + gnb_ref[...]
    o_ref[...] = ((yn + bonus) * gate).astype(BF16)


def _rwkv_mix(xs3, dbase, dup, ibase, iup, gup, keyk, keya, bonus, gnw, gnb, seg, tri, tc):
    bsz, seq, ncol = xs3.shape
    const = lambda b, i: (0, 0)
    vec = pl.BlockSpec((1, BRANCH_WIDTH), const)
    return pl.pallas_call(
        functools.partial(_rwkv_kernel, n_sub=tc // CHUNK),
        grid=(bsz, seq // tc),
        in_specs=[
            pl.BlockSpec((None, tc, ncol), lambda b, i: (b, i, 0)),
            vec,
            pl.BlockSpec(dup.shape, const),
            vec,
            pl.BlockSpec(iup.shape, const),
            pl.BlockSpec(gup.shape, const),
            vec, vec, vec, vec, vec,
            pl.BlockSpec(seg.shape, const),
            pl.BlockSpec(tri.shape, const),
        ],
        out_specs=pl.BlockSpec((None, tc, BRANCH_WIDTH), lambda b, i: (b, i, 0)),
        out_shape=jax.ShapeDtypeStruct((bsz, seq, BRANCH_WIDTH), BF16),
        scratch_shapes=[pltpu.VMEM((N_GROUPS, LANES, LANES), F32)],
        compiler_params=pltpu.CompilerParams(
            dimension_semantics=("parallel", "arbitrary"), vmem_limit_bytes=VMEM_LIMIT),
        name="rwkv_mix",
    )(xs3, dbase, dup, ibase, iup, gup, keyk, keya, bonus, gnw, gnb, seg, tri)


def _merge_kernel(x_ref, osb_ref, orw_ref, pg_ref, bg_ref, wb0_ref, wb1_ref, wout_ref,
                  g2_ref, x1_ref, h2_ref):
    d = x_ref.shape[1]
    n_sub = 4
    sub = x_ref.shape[0] // n_sub
    rows = [slice(s * sub, (s + 1) * sub) for s in range(n_sub)]
    ups = [(_dot(osb_ref[r, :], wb0_ref[...]), _dot(orw_ref[r, :], wb1_ref[...])) for r in rows]
    mixed = []
    for r, (up0, up1) in zip(rows, ups):
        gates = _sigmoid(pg_ref[r, :].astype(F32) + bg_ref[...])
        mixed.append((gates[:, :d] * up0 + gates[:, d:] * up1).astype(BF16))
    outs = [_dot(m, wout_ref[...]) for m in mixed]
    for r, out in zip(rows, outs):
        x1 = x_ref[r, :] + out
        x1_ref[r, :] = x1
        h2 = x1 * lax.rsqrt(jnp.mean(x1 * x1, axis=-1, keepdims=True) + NORM_EPS) * g2_ref[...]
        h2_ref[r, :] = h2.astype(BF16)


def _merge(x2, o_sb, o_rw, p_gate, bg, wb0, wb1, w_out, g2, tm):
    t, d = x2.shape
    const = lambda i: (0, 0)
    row = lambda i: (i, 0)
    return pl.pallas_call(
        _merge_kernel,
        grid=(t // tm,),
        in_specs=[
            pl.BlockSpec((tm, d), row),
            pl.BlockSpec((tm, BRANCH_WIDTH), row),
            pl.BlockSpec((tm, BRANCH_WIDTH), row),
            pl.BlockSpec((tm, 2 * d), row),
            pl.BlockSpec((1, 2 * d), const),
            pl.BlockSpec(wb0.shape, const),
            pl.BlockSpec(wb1.shape, const),
            pl.BlockSpec(w_out.shape, const),
            pl.BlockSpec((1, d), const),
        ],
        out_specs=[pl.BlockSpec((tm, d), row), pl.BlockSpec((tm, d), row)],
        out_shape=[jax.ShapeDtypeStruct((t, d), F32), jax.ShapeDtypeStruct((t, d), BF16)],
        compiler_params=pltpu.CompilerParams(
            dimension_semantics=("parallel",), vmem_limit_bytes=VMEM_LIMIT),
        name="merge_outproj",
    )(x2, o_sb, o_rw, p_gate, bg, wb0, wb1, w_out, g2)


def _ffn_kernel(h2_ref, x1_ref, wup_ref, cw_ref, cb_ref, wdown_ref, o_ref, carry_ref,
                *, tiles_per_seq, d_ff, nc, sub):
    i = pl.program_id(0)
    first_tile = (i % tiles_per_seq) == 0
    tm = h2_ref.shape[0]
    n_sub = tm // sub
    n_ch = d_ff // nc
    row = lax.broadcasted_iota(jnp.int32, (sub, nc), 0)
    rows = [slice(s * sub, (s + 1) * sub) for s in range(n_sub)]
    h2s = [h2_ref[r, :] for r in rows]

    def conv(u, col0, seq_start):
        cs = slice(col0, col0 + nc)
        old = carry_ref[:, cs]
        if seq_start is not None:
            old = jnp.where(seq_start, 0.0, old)
        p1 = jnp.where(row == 0, old[7:8, :], pltpu.roll(u, 1, 0))
        p2 = jnp.where(row == 0, old[6:7, :],
                       jnp.where(row == 1, old[7:8, :], pltpu.roll(u, 2, 0)))
        carry_ref[:, cs] = u[sub - 8:, :]
        cw = cw_ref[:, cs]
        return cw[0:1] * p2 + cw[1:2] * p1 + cw[2:3] * u + cb_ref[:, cs]

    def up(item):
        s, ch = item
        return (_dot(h2s[s], wup_ref[:, ch * nc:(ch + 1) * nc]),
                _dot(h2s[s], wup_ref[:, d_ff + ch * nc:d_ff + (ch + 1) * nc]))

    items = [(s, ch) for s in range(n_sub) for ch in range(n_ch)]
    accs = [x1_ref[r, :] for r in rows]
    ahead = 3
    pending = [up(it) for it in items[:ahead]]
    for k, (s, ch) in enumerate(items):
        ug, uv = pending.pop(0)
        if k + ahead < len(items):
            pending.append(up(items[k + ahead]))
        seq_start = first_tile if s == 0 else None
        cg = conv(ug, ch * nc, seq_start)
        cv = conv(uv, d_ff + ch * nc, seq_start)
        act = cg * _sigmoid(cg) * cv
        accs[s] = accs[s] + _dot(act.astype(BF16), wdown_ref[ch * nc:(ch + 1) * nc, :])
    for s, r in enumerate(rows):
        o_ref[r, :] = accs[s]


def _ffn(h2, x1, w_up, conv_w, conv_b, w_down, seq, tm, nc, sub):
    t, d = x1.shape
    d_ff = w_down.shape[0]
    const = lambda i: (0, 0)
    row = lambda i: (i, 0)
    return pl.pallas_call(
        functools.partial(_ffn_kernel, tiles_per_seq=seq // tm, d_ff=d_ff, nc=nc, sub=sub),
        grid=(t // tm,),
        in_specs=[
            pl.BlockSpec((tm, d), row),
            pl.BlockSpec((tm, d), row),
            pl.BlockSpec(w_up.shape, const),
            pl.BlockSpec(conv_w.shape, const),
            pl.BlockSpec(conv_b.shape, const),
            pl.BlockSpec(w_down.shape, const),
        ],
        out_specs=pl.BlockSpec((tm, d), row),
        out_shape=jax.ShapeDtypeStruct((t, d), F32),
        scratch_shapes=[pltpu.VMEM((8, 2 * d_ff), F32)],
        compiler_params=pltpu.CompilerParams(
            dimension_semantics=("arbitrary",), vmem_limit_bytes=VMEM_LIMIT),
        name="ffn",
    )(h2, x1, w_up, conv_w, conv_b, w_down)


def _pad_cols(w, n):
    return jnp.pad(w, ((0, 0), (0, n - w.shape[1])))


def _pad_rows(w, n):
    return jnp.pad(w, ((0, n - w.shape[0]), (0, 0)))


def _layer(x, attn_norm_g, w_in, q_norm_g, k_norm_g, rwkv_shift_mu, decay_base, decay_up,
           iclr_base, iclr_up, out_gate_up, key_k, key_a, bonus_rk, group_norm_w,
           group_norm_b, branch_gate_b, w_branch, w_out, ffn_norm_g, w_ffn_up,
           ffn_conv_w, ffn_conv_b, w_ffn_down):
    bsz, seq, d = x.shape
    t = bsz * seq
    bw = BRANCH_WIDTH
    sb_cols = 3 * bw
    rw_cols = 3 * bw + DECAY_LORA + ICLR_LORA + GATE_LORA
    x2 = x.reshape(t, d)

    w_sb = w_in[:, :sb_cols].astype(BF16)
    w_rw_raw = w_in[:, sb_cols:sb_cols + rw_cols]
    mu_raw = rwkv_shift_mu.reshape(1, rw_cols)

    def regroup(m):
        o = 3 * bw
        return jnp.concatenate([
            m[:, :o],
            _pad_cols(m[:, o:o + DECAY_LORA], LORA_W_PAD),
            _pad_cols(m[:, o + DECAY_LORA:o + DECAY_LORA + ICLR_LORA], LORA_A_PAD),
            _pad_cols(m[:, o + DECAY_LORA + ICLR_LORA:], LORA_G_PAD),
        ], axis=1)

    w_rw = regroup(w_rw_raw).astype(BF16)
    mu = regroup(mu_raw)
    w_gate = w_in[:, sb_cols + rw_cols:].astype(BF16)

    qg_pair = jnp.tile(q_norm_g.reshape(1, HEAD_DIM), (1, HEADS_PER_GROUP))
    kg_pair = jnp.tile(k_norm_g.reshape(1, HEAD_DIM), (1, HEADS_PER_GROUP))
    p_sb, xs, p_gate = _in_projection(
        x2, attn_norm_g.reshape(1, d), w_sb, w_rw, w_gate, mu, qg_pair, kg_pair, seq,
        tm=512, sub=256)

    blk = 256
    tri_sb = (jnp.arange(blk)[:, None] > jnp.arange(blk)[None, :]).astype(BF16)
    o_sb = _sb_attention(p_sb.reshape(bsz, seq, sb_cols), tri_sb, blk, gps=4)

    head_id = jnp.arange(LANES) // HEAD_DIM
    seg = (head_id[:, None] == head_id[None, :]).astype(BF16)
    tri_rw = (jnp.arange(CHUNK)[:, None] >= jnp.arange(CHUNK)[None, :]).astype(BF16)
    vec = lambda p: p.reshape(1, bw)
    o_rw = _rwkv_mix(
        xs.reshape(bsz, seq, RW_COLS_PAD), vec(decay_base), _pad_rows(decay_up, LORA_W_PAD),
        vec(iclr_base), _pad_rows(iclr_up, LORA_A_PAD), _pad_rows(out_gate_up, LORA_G_PAD),
        vec(key_k), vec(key_a), vec(bonus_rk), vec(group_norm_w), vec(group_norm_b),
        seg, tri_rw, tc=512)

    x1, h2 = _merge(
        x2, o_sb.reshape(t, bw), o_rw.reshape(t, bw), p_gate,
        branch_gate_b.reshape(1, 2 * d), w_branch[0].astype(BF16), w_branch[1].astype(BF16),
        w_out.astype(BF16), ffn_norm_g.reshape(1, d), tm=1024)

    out = _ffn(h2, x1, w_ffn_up.astype(BF16), ffn_conv_w, ffn_conv_b.reshape(1, -1),
               w_ffn_down.astype(BF16), seq, tm=512, nc=256, sub=256)
    return out.reshape(bsz, seq, d)


def kernel(x, attn_norm_g, w_in, q_norm_g, k_norm_g, rwkv_shift_mu, decay_base, decay_up,
           iclr_base, iclr_up, out_gate_up, key_k, key_a, bonus_rk, group_norm_w,
           group_norm_b, branch_gate_b, w_branch, w_out, ffn_norm_g, w_ffn_up, ffn_conv_w,
           ffn_conv_b, w_ffn_down):
    for layer in range(attn_norm_g.shape[0]):
        x = _layer(
            x, attn_norm_g[layer], w_in[layer], q_norm_g[layer], k_norm_g[layer],
            rwkv_shift_mu[layer], decay_base[layer], decay_up[layer], iclr_base[layer],
            iclr_up[layer], out_gate_up[layer], key_k[layer], key_a[layer], bonus_rk[layer],
            group_norm_w[layer], group_norm_b[layer], branch_gate_b[layer], w_branch[layer],
            w_out[layer], ffn_norm_g[layer], w_ffn_up[layer], ffn_conv_w[layer],
            ffn_conv_b[layer], w_ffn_down[layer])
    return x
```
